```python
import jax
import jax.numpy as jnp
from jax import lax
import numpy as np

D_MODEL = 1024
BATCH = 32
SEQ = 2048
DEPTH = 2

CHUNK = 64
PLE_DIM = 256
NORM_EPS = 1e-6
N_EVEN = (DEPTH + 1) // 2
N_ODD = DEPTH // 2
A_HEADS = 8
A_HEAD_DIM = 64
A_KV_DIM = 64
A_WIDTH = A_HEADS * A_HEAD_DIM
IDX_HEADS = 4
IDX_DIM = 64
TOPK_MAX = 256
B_HEADS = 8
B_HEAD_DIM = 64
B_WIDTH = B_HEADS * B_HEAD_DIM
DECAY_LORA = 64
AAA_LORA = 64
GATE_LORA = 128
RWKV_GN_EPS = 64e-5
A_COLS = (A_WIDTH, A_KV_DIM, A_KV_DIM, IDX_HEADS * IDX_DIM, IDX_DIM, IDX_HEADS)
B_COLS = (B_WIDTH, B_WIDTH, B_WIDTH, DECAY_LORA, AAA_LORA, GATE_LORA)
A_TOTAL = sum(A_COLS)
B_TOTAL = sum(B_COLS)
MIX_WIDTH = A_WIDTH + B_WIDTH
C_INNER = 2 * D_MODEL
C_HEADS = 4
C_HEAD_DIM = C_INNER // C_HEADS
C_CONV = 4
C_QKV_BLOCK = 4
C_GN_EPS = 1e-5
FFN_HIDDEN = -(-(8 * D_MODEL) // (3 * 256)) * 256

kernel_name = "hybrid_dsa_rwkv7_mlstm_trunk"


def _split_cols(u, sizes):
    cuts = [int(c) for c in np.cumsum(sizes)[:-1]]
    return jnp.split(u, cuts, axis=-1)


def rms_norm(x, g, eps=NORM_EPS):
    xf = x.astype(jnp.float32)
    y = xf * lax.rsqrt(jnp.mean(xf * xf, axis=-1, keepdims=True) + eps)
    return (y * g).astype(x.dtype)


def head_norm(y, eps):
    y = y.astype(jnp.float32)
    mean = jnp.mean(y, axis=-1, keepdims=True)
    var = jnp.mean(jnp.square(y - mean), axis=-1, keepdims=True)
    return (y - mean) * lax.rsqrt(var + eps)


def token_shift(u, mu):
    prev = jnp.pad(u, ((0, 0), (1, 0), (0, 0)))[:, :-1]
    return u + mu * (prev - u)


def dsa_attention(q, k, v, iq, ik, iw, q_gain, k_gain):
    B_, S_ = q.shape[0], q.shape[1]
    n_chunks = S_ // CHUNK
    k_sel = min(TOPK_MAX, S_ // 4)
    q = rms_norm(q, q_gain)
    k = rms_norm(k, k_gain)
    iw = iw * (IDX_HEADS ** -0.5 * IDX_DIM ** -0.5)
    key_pos = jnp.arange(S_)

    def to_chunks(t):
        return jnp.moveaxis(t.reshape((B_, n_chunks, CHUNK) + t.shape[2:]), 1, 0)

    def one_chunk(args):
        c, qc, iqc, iwc = args
        limit = (c + 1) * CHUNK
        logits = jnp.einsum('bthd,bsd->bths', iqc, ik, preferred_element_type=jnp.float32)
        score = jnp.einsum('bth,bths->bts', iwc.astype(jnp.float32), jax.nn.relu(logits))
        score = jnp.where(key_pos[None, None, :] < limit, score, -jnp.inf)
        _, idx = lax.top_k(score, k_sel)
        valid = idx < limit
        kg = jax.vmap(lambda kb, ib: kb[ib])(k, idx)
        vg = jax.vmap(lambda vb, ib: vb[ib])(v, idx)
        s = jnp.einsum('bthd,btjd->bthj', qc, kg, preferred_element_type=jnp.float32) * (A_HEAD_DIM ** -0.5)
        s = jnp.where(valid[:, :, None, :], s, -jnp.inf)
        pr = jax.nn.softmax(s, axis=-1)
        return jnp.einsum('bthj,btjd->bthd', pr.astype(vg.dtype), vg)

    out = lax.map(one_chunk, (jnp.arange(n_chunks), to_chunks(q), to_chunks(iq), to_chunks(iw)))
    return jnp.moveaxis(out, 0, 1).reshape(B_, S_, A_WIDTH)


def rwkv7_time_mix(u, mu, w0, w2, a0, a2, g2, k_k, k_a, r_k, gn_g, gn_b):
    B_, S_ = u.shape[0], u.shape[1]
    u = token_shift(u, mu)
    r, k, v, xw, xa, xg = _split_cols(u, B_COLS)
    w_raw = -jax.nn.softplus(-(w0 + jnp.tanh(xw) @ w2)) - 0.5
    decay = jnp.exp(-jnp.exp(w_raw.astype(jnp.float32)))
    a = jax.nn.sigmoid(a0 + xa @ a2)
    g = jax.nn.sigmoid(xg) @ g2
    heads = lambda t: t.astype(jnp.float32).reshape(B_, S_, B_HEADS, B_HEAD_DIM)
    kk = heads(k * k_k)
    kk = kk / jnp.maximum(jnp.linalg.norm(kk, axis=-1, keepdims=True), 1e-12)
    k = k * (1.0 + (a - 1.0) * k_a)
    r_h, k_h, v_h, a_h, w_h = heads(r), heads(k), heads(v), heads(a), heads(decay)

    def step(state, inp):
        r_t, w_t, k_t, v_t, kk_t, a_t = inp
        sa = jnp.einsum('bhvk,bhk->bhv', state, -kk_t)
        state = (state * w_t[:, :, None, :] + sa[..., None] * (kk_t * a_t)[:, :, None, :]
                 + v_t[..., None] * k_t[:, :, None, :])
        return state, jnp.einsum('bhvk,bhk->bhv', state, r_t)

    xs = tuple(jnp.moveaxis(t, 1, 0) for t in (r_h, w_h, k_h, v_h, kk, a_h))
    state0 = jnp.zeros((B_, B_HEADS, B_HEAD_DIM, B_HEAD_DIM), jnp.float32)
    _, y = lax.scan(step, state0, xs)
    y = jnp.moveaxis(y, 0, 1)
    y = head_norm(y, RWKV_GN_EPS).reshape(B_, S_, B_WIDTH) * gn_g + gn_b
    bonus = jnp.sum(r_h * k_h * r_k, axis=-1, keepdims=True) * v_h
    y = (y + bonus.reshape(B_, S_, B_WIDTH)) * g
    return y.astype(u.dtype)


def mlstm_chunkwise(q, k, v, i_pre, f_pre):
    B_, S_, H, Dh = q.shape
    n_chunks = S_ // CHUNK
    f32 = jnp.float32
    cq = lambda t: t.astype(f32).reshape(B_, n_chunks, CHUNK, H, Dh).transpose(1, 0, 3, 2, 4)
    cg = lambda t: t.astype(f32).reshape(B_, n_chunks, CHUNK, H).transpose(1, 0, 3, 2)
    log_f = jax.nn.log_sigmoid(f_pre.astype(f32))
    tril = jnp.tril(jnp.ones((CHUNK, CHUNK), dtype=bool))

    def step(carry, inp):
        C, nv, m = carry
        qc, kc, vc, ic, lfc = inp
        b = jnp.cumsum(lfc, axis=-1)
        d = b[..., :, None] - b[..., None, :] + ic[..., None, :]
        d = jnp.where(tril, d, -jnp.inf)
        inter = b + m[..., None]
        m_t = jnp.maximum(inter, jnp.max(d, axis=-1))
        wts = jnp.exp(d - m_t[..., None])
        sc = jnp.exp(inter - m_t)
        qk = jnp.einsum('bhtd,bhsd->bhts', qc, kc) * wts
        num = jnp.einsum('bhts,bhsd->bhtd', qk, vc) + sc[..., None] * jnp.einsum('bhtd,bhde->bhte', qc, C)
        den = jnp.sum(qk, axis=-1) + sc * jnp.einsum('bhtd,bhd->bht', qc, nv)
        h = num / jnp.maximum(jnp.abs(den), jnp.exp(-m_t))[..., None]
        b_last = b[..., -1]
        g_s = b_last[..., None] - b + ic
        m_new = jnp.maximum(b_last + m, jnp.max(g_s, axis=-1))
        dc = jnp.exp(b_last + m - m_new)
        ws = jnp.exp(g_s - m_new[..., None])
        C = dc[..., None, None] * C + jnp.einsum('bhs,bhsd,bhse->bhde', ws, kc, vc)
        nv = dc[..., None] * nv + jnp.einsum('bhs,bhsd->bhd', ws, kc)
        return (C, nv, m_new), h

    carry0 = (jnp.zeros((B_, H, Dh, Dh), f32), jnp.zeros((B_, H, Dh), f32), jnp.zeros((B_, H), f32))
    _, hs = lax.scan(step, carry0, (cq(q), cq(k), cq(v), cg(i_pre), cg(log_f)))
    return hs.transpose(1, 0, 3, 2, 4).reshape(B_, S_, H, Dh)


def even_mixer(h, w_in, a_q_gain, a_k_gain, b_mu, b_w0, b_w2, b_a0, b_a2, b_g2, b_k_k, b_k_a,
               b_r_k, b_gn_g, b_gn_b, w_out):
    B_, S_ = h.shape[0], h.shape[1]
    u = h @ w_in
    qa, ka, va, iq, ik, iw = _split_cols(u[..., :A_TOTAL], A_COLS)
    ya = dsa_attention(qa.reshape(B_, S_, A_HEADS, A_HEAD_DIM), ka, va,
                       iq.reshape(B_, S_, IDX_HEADS, IDX_DIM), ik, iw, a_q_gain, a_k_gain)
    yb = rwkv7_time_mix(u[..., A_TOTAL:], b_mu, b_w0, b_w2, b_a0, b_a2, b_g2, b_k_k, b_k_a,
                        b_r_k, b_gn_g, b_gn_b)
    return jnp.concatenate([ya, yb.astype(ya.dtype)], axis=-1) @ w_out


def odd_mixer(h, w_up, conv_w, conv_b, wq, wk, wv, w_if, b_i, b_f, mh_g, skip, w_down):
    B_, S_ = h.shape[0], h.shape[1]
    xm, z = jnp.split(h @ w_up, 2, axis=-1)
    xc = lax.conv_general_dilated(xm, conv_w, (1,), [(C_CONV - 1, 0)],
                                  dimension_numbers=('NWC', 'WIO', 'NWC'),
                                  feature_group_count=C_INNER) + conv_b
    xc = jax.nn.silu(xc)

    def headwise(t, w):
        tb = t.reshape(B_, S_, C_INNER // C_QKV_BLOCK, C_QKV_BLOCK)
        return jnp.einsum('bsgi,gij->bsgj', tb, w).reshape(B_, S_, C_INNER)

    q = headwise(xc, wq)
    k = headwise(xc, wk)
    v = headwise(xm, wv)
    gates = jnp.concatenate([q, k, v], axis=-1) @ w_if
    i_pre = gates[..., :C_HEADS] + b_i
    f_pre = gates[..., C_HEADS:] + b_f
    hh = lambda t: t.reshape(B_, S_, C_HEADS, C_HEAD_DIM)
    y = mlstm_chunkwise(hh(q), hh(k) * (C_HEAD_DIM ** -0.5), hh(v), i_pre, f_pre)
    y = head_norm(y, C_GN_EPS).reshape(B_, S_, C_INNER) * mh_g
    y = (y + skip * xc) * jax.nn.silu(z)
    return y.astype(h.dtype) @ w_down


def swiglu_ffn(x, g, w_gate, w_up, w_down):
    h = rms_norm(x, g)
    return (jax.nn.silu(h @ w_gate) * (h @ w_up)) @ w_down


def per_layer_embedding(x, p_i, w_ple, g, w_gate):
    gate = jax.nn.sigmoid(rms_norm(x, g) @ w_gate)
    return (p_i @ w_ple) * gate


def setup_inputs(seed: int = 0) -> dict:
    key = jax.random.key(seed)
    ks = iter(jax.random.split(key, 40))
    nrm = lambda shape, scale: jax.random.normal(next(ks), shape, jnp.float32) * scale
    gain = lambda shape: 1.0 + 0.02 * jax.random.normal(next(ks), shape, jnp.float32)
    D = D_MODEL
    return {
        'x': nrm((BATCH, SEQ, D), 1.0),
        'p': nrm((DEPTH, BATCH, SEQ, PLE_DIM), 1.0),
        'mix_norm': gain((DEPTH, D)),
        'a_q_gain': gain((N_EVEN, A_HEAD_DIM)),
        'a_k_gain': gain((N_EVEN, A_KV_DIM)),
        'w_in_e': nrm((N_EVEN, D, A_TOTAL + B_TOTAL), D ** -0.5),
        'b_mu': jax.random.uniform(next(ks), (N_EVEN, B_TOTAL), jnp.float32),
        'b_w0': jax.random.uniform(next(ks), (N_EVEN, B_WIDTH), jnp.float32, minval=-5.0, maxval=0.5),
        'b_w2': nrm((N_EVEN, DECAY_LORA, B_WIDTH), 0.1),
        'b_a0': nrm((N_EVEN, B_WIDTH), 0.1),
        'b_a2': nrm((N_EVEN, AAA_LORA, B_WIDTH), 0.5 * AAA_LORA ** -0.5),
        'b_g2': nrm((N_EVEN, GATE_LORA, B_WIDTH), GATE_LORA ** -0.5),
        'b_k_k': 0.85 + 0.02 * jax.random.normal(next(ks), (N_EVEN, B_WIDTH), jnp.float32),
        'b_k_a': gain((N_EVEN, B_WIDTH)),
        'b_r_k': nrm((N_EVEN, B_HEADS, B_HEAD_DIM), 0.1),
        'b_gn_g': gain((N_EVEN, B_WIDTH)),
        'b_gn_b': nrm((N_EVEN, B_WIDTH), 0.01),
        'w_out_e': nrm((N_EVEN, MIX_WIDTH, D), MIX_WIDTH ** -0.5),
        'c_w_up': nrm((N_ODD, D, 2 * C_INNER), D ** -0.5),
        'c_conv_w': nrm((N_ODD, C_CONV, 1, C_INNER), C_CONV ** -0.5),
        'c_conv_b': nrm((N_ODD, C_INNER), 0.01),
        'c_wq': nrm((N_ODD, C_INNER // C_QKV_BLOCK, C_QKV_BLOCK, C_QKV_BLOCK), C_QKV_BLOCK ** -0.5),
        'c_wk': nrm((N_ODD, C_INNER // C_QKV_BLOCK, C_QKV_BLOCK, C_QKV_BLOCK), C_QKV_BLOCK ** -0.5),
        'c_wv': nrm((N_ODD, C_INNER // C_QKV_BLOCK, C_QKV_BLOCK, C_QKV_BLOCK), C_QKV_BLOCK ** -0.5),
        'c_w_if': nrm((N_ODD, 3 * C_INNER, 2 * C_HEADS), 0.5 * (3 * C_INNER) ** -0.5),
        'c_b_i': nrm((N_ODD, C_HEADS), 0.1),
        'c_b_f': jnp.linspace(3.0, 6.0, C_HEADS)[None, :] + nrm((N_ODD, C_HEADS), 0.1),
        'c_mh_g': gain((N_ODD, C_INNER)),
        'c_skip': gain((N_ODD, C_INNER)),
        'c_w_down': nrm((N_ODD, C_INNER, D), C_INNER ** -0.5),
        'ffn_norm': gain((DEPTH, D)),
        'ffn_w_gate': nrm((DEPTH, D, FFN_HIDDEN), D ** -0.5),
        'ffn_w_up': nrm((DEPTH, D, FFN_HIDDEN), D ** -0.5),
        'ffn_w_down': nrm((DEPTH, FFN_HIDDEN, D), FFN_HIDDEN ** -0.5),
        'ple_w': nrm((DEPTH, PLE_DIM, D), PLE_DIM ** -0.5),
        'ple_norm': gain((DEPTH, D)),
        'ple_w_gate': nrm((DEPTH, D, D), D ** -0.5),
    }


def reference(x, p, mix_norm, a_q_gain, a_k_gain, w_in_e, b_mu, b_w0, b_w2, b_a0, b_a2, b_g2,
              b_k_k, b_k_a, b_r_k, b_gn_g, b_gn_b, w_out_e, c_w_up, c_conv_w, c_conv_b, c_wq,
              c_wk, c_wv, c_w_if, c_b_i, c_b_f, c_mh_g, c_skip, c_w_down, ffn_norm, ffn_w_gate,
              ffn_w_up, ffn_w_down, ple_w, ple_norm, ple_w_gate):
    for i in range(DEPTH):
        j = i // 2
        h = rms_norm(x, mix_norm[i])
        if i % 2 == 0:
            x = x + even_mixer(h, w_in_e[j], a_q_gain[j], a_k_gain[j], b_mu[j], b_w0[j], b_w2[j],
                               b_a0[j], b_a2[j], b_g2[j], b_k_k[j], b_k_a[j], b_r_k[j],
                               b_gn_g[j], b_gn_b[j], w_out_e[j])
        else:
            x = x + odd_mixer(h, c_w_up[j], c_conv_w[j], c_conv_b[j], c_wq[j], c_wk[j], c_wv[j],
                              c_w_if[j], c_b_i[j], c_b_f[j], c_mh_g[j], c_skip[j], c_w_down[j])
        x = x + swiglu_ffn(x, ffn_norm[i], ffn_w_gate[i], ffn_w_up[i], ffn_w_down[i])
        x = x + per_layer_embedding(x, p[i], ple_w[i], ple_norm[i], ple_w_gate[i])
    return x
```

```python
import functools

import jax
import jax.numpy as jnp
from jax import lax
from jax.experimental import pallas as pl
from jax.experimental.pallas import tpu as pltpu

F32 = jnp.float32
BF16 = jnp.bfloat16

D_MODEL = 1024
CHUNK = 64
PLE_DIM = 256
NORM_EPS = 1e-6
A_HEADS = 8
A_HEAD_DIM = 64
A_KV_DIM = 64
A_WIDTH = A_HEADS * A_HEAD_DIM
IDX_HEADS = 4
IDX_DIM = 64
TOPK_MAX = 256
B_HEADS = 8
B_HEAD_DIM = 64
B_WIDTH = B_HEADS * B_HEAD_DIM
DECAY_LORA = 64
AAA_LORA = 64
GATE_LORA = 128
RWKV_GN_EPS = 64e-5
A_TOTAL = A_WIDTH + 2 * A_KV_DIM + IDX_HEADS * IDX_DIM + IDX_DIM + IDX_HEADS
B_TOTAL = 3 * B_WIDTH + DECAY_LORA + AAA_LORA + GATE_LORA
A_PAD = 1024
C_INNER = 2 * D_MODEL
C_HEADS = 4
C_HEAD_DIM = C_INNER // C_HEADS
C_CONV = 4
C_QKV_BLOCK = 4
C_GN_EPS = 1e-5
FFN_HIDDEN = 2816
QKV_TILE = 256
GATE_PAD = 128

VMEM_LIMIT = 56 * 1024 * 1024


def _params(*sem):
    return pltpu.CompilerParams(dimension_semantics=sem, vmem_limit_bytes=VMEM_LIMIT)


def _resident(shape):
    nd = len(shape)
    return pl.BlockSpec(shape, lambda *_: (0,) * nd, pipeline_mode=pl.Buffered(1))


def _rms(x, g):
    return x * lax.rsqrt(jnp.mean(x * x, axis=-1, keepdims=True) + NORM_EPS) * g


def _dot(a, b):
    return jnp.dot(a.astype(BF16), b.astype(BF16), preferred_element_type=F32)


def _dot_nt(a, b):
    return lax.dot_general(a.astype(BF16), b.astype(BF16), (((1,), (1,)), ((), ())),
                           preferred_element_type=F32)


def _dot_tn(a, b):
    return lax.dot_general(a.astype(BF16), b.astype(BF16), (((0,), (0,)), ((), ())),
                           preferred_element_type=F32)


def _split3(x):
    hi = x.astype(BF16)
    r1 = x - hi.astype(F32)
    mid = r1.astype(BF16)
    lo = (r1 - mid.astype(F32)).astype(BF16)
    return hi, mid, lo


def _dot_exact_lhs(a_exact, x):
    a = a_exact.astype(BF16)
    hi, mid, lo = _split3(x)
    return (jnp.dot(a, hi, preferred_element_type=F32) + jnp.dot(a, mid, preferred_element_type=F32)
            + jnp.dot(a, lo, preferred_element_type=F32))


def _dot_exact_rhs(x, b_exact):
    b = b_exact.astype(BF16)
    hi, mid, lo = _split3(x)
    return (jnp.dot(hi, b, preferred_element_type=F32) + jnp.dot(mid, b, preferred_element_type=F32)
            + jnp.dot(lo, b, preferred_element_type=F32))


def _sigmoid(x):
    return 1.0 / (1.0 + jnp.exp(-x))


def _softplus(x):
    return jnp.maximum(x, 0.0) + jnp.log(1.0 + jnp.exp(-jnp.abs(x)))


def _silu(x):
    return x * _sigmoid(x)


def _iota(shape, dim):
    return lax.broadcasted_iota(jnp.int32, shape, dim)


def _group_ones(n, group):
    return (_iota((n, n), 0) // group == _iota((n, n), 1) // group).astype(F32)


def _inproj_kernel(x_ref, g_ref, wa_ref, wb_ref, ua_ref, ub_ref):
    h = _rms(x_ref[...], g_ref[...]).astype(BF16)
    ua_ref[...] = jnp.dot(h, wa_ref[...], preferred_element_type=F32)
    ub_ref[...] = jnp.dot(h, wb_ref[...], preferred_element_type=F32)


def _inproj(x, g, wa, wb, tm):
    t = x.shape[0]
    return pl.pallas_call(
        _inproj_kernel,
        grid=(t // tm,),
        in_specs=[pl.BlockSpec((tm, D_MODEL), lambda i: (i, 0)), _resident((1, D_MODEL)),
                  _resident(wa.shape), _resident(wb.shape)],
        out_specs=[pl.BlockSpec((tm, A_PAD), lambda i: (i, 0)),
                   pl.BlockSpec((tm, B_TOTAL), lambda i: (i, 0))],
        out_shape=[jax.ShapeDtypeStruct((t, A_PAD), F32), jax.ShapeDtypeStruct((t, B_TOTAL), F32)],
        compiler_params=_params("parallel"), name="inproj",
    )(x, g, wa, wb)


def _dsa_kernel(q_ref, iq_ref, iwq_ref, kv_ref, ikw_ref, qg_ref, kg_ref, out_ref, key_scr, sel_scr,
                *, tq, s, k_sel):
    qi = pl.program_id(1)
    lane = 128
    nblk = s // lane

    ik = ikw_ref[:, :IDX_DIM].astype(BF16)
    iw = iwq_ref[:, IDX_DIM:IDX_DIM + IDX_HEADS] * (IDX_HEADS ** -0.5 * IDX_DIM ** -0.5)
    iq = iq_ref[...]
    score = jnp.zeros((tq, s), F32)
    for h in range(IDX_HEADS):
        lg = _dot_nt(iq[:, h * IDX_DIM:(h + 1) * IDX_DIM], ik)
        score = score + iw[:, h:h + 1] * jnp.maximum(lg, 0.0)
    row = _iota((tq, 1), 0) + qi * tq
    limit = (jnp.right_shift(row, CHUNK.bit_length() - 1) + 1) * CHUNK
    valid = _iota((tq, s), 1) < limit
    score = jnp.where(valid, score + 0.0, -jnp.inf)

    bits = pltpu.bitcast(score, jnp.int32)
    key_scr[...] = jnp.where(bits < 0, bits ^ jnp.int32(0x7FFFFFFF), bits)

    kf = jnp.float32(k_sel)

    def count_ge(cand):
        return jnp.sum(jnp.where(key_scr[...] >= cand, 1.0, 0.0), axis=-1, keepdims=True)

    int_min = jnp.int32(-2 ** 31)
    ans = jnp.where(count_ge(jnp.zeros((tq, 1), jnp.int32)) >= kf, jnp.int32(0), int_min)

    def bit_step(i, ans):
        cand = ans | jnp.left_shift(jnp.int32(1), jnp.int32(30) - i)
        return jnp.where(count_ge(cand) >= kf, cand, ans)

    thr = lax.fori_loop(0, 31, bit_step, ans)

    key = key_scr[...]
    gt = key > thr
    need = kf - jnp.sum(jnp.where(gt, 1.0, 0.0), axis=-1, keepdims=True)
    tri = (_iota((lane, lane), 0) <= _iota((lane, lane), 1)).astype(BF16)
    carry = jnp.zeros((tq, 1), F32)
    for j in range(nblk):
        cols = slice(j * lane, (j + 1) * lane)
        eq_j = key[:, cols] == thr
        rank = carry + jnp.dot(jnp.where(eq_j, 1.0, 0.0).astype(BF16), tri, preferred_element_type=F32)
        sel_j = (gt[:, cols] | (eq_j & (rank <= need))) & valid[:, cols]
        sel_scr[:, cols] = jnp.where(sel_j, 0.0, -jnp.inf)
        carry = rank[:, lane - 1:lane]

    kv = kv_ref[...]
    kraw = kv[:, :A_KV_DIM]
    kn = (kraw * lax.rsqrt(jnp.mean(kraw * kraw, axis=-1, keepdims=True) + NORM_EPS) * kg_ref[...]).astype(BF16)
    v = kv[:, A_KV_DIM:2 * A_KV_DIM].astype(BF16)
    q = q_ref[...]
    neg = sel_scr[...]
    for h in range(A_HEADS):
        qh = q[:, h * A_HEAD_DIM:(h + 1) * A_HEAD_DIM]
        qn = qh * lax.rsqrt(jnp.mean(qh * qh, axis=-1, keepdims=True) + NORM_EPS) * qg_ref[...]
        sc = _dot_nt(qn, kn) * (A_HEAD_DIM ** -0.5) + neg
        p = jnp.exp(sc - jnp.max(sc, axis=-1, keepdims=True))
        p = p / jnp.sum(p, axis=-1, keepdims=True)
        out_ref[:, h * A_HEAD_DIM:(h + 1) * A_HEAD_DIM] = jnp.dot(p.astype(BF16), v, preferred_element_type=F32)


def _dsa(ua, q_gain, k_gain, b, s, tq):
    t = b * s
    nq = s // tq
    k_sel = min(TOPK_MAX, s // 4)
    kern = functools.partial(_dsa_kernel, tq=tq, s=s, k_sel=k_sel)
    return pl.pallas_call(
        kern,
        grid=(b, nq),
        in_specs=[
            pl.BlockSpec((tq, A_WIDTH), lambda bi, qi: (bi * nq + qi, 0)),
            pl.BlockSpec((tq, 256), lambda bi, qi: (bi * nq + qi, 2)),
            pl.BlockSpec((tq, 128), lambda bi, qi: (bi * nq + qi, 7)),
            pl.BlockSpec((s, 128), lambda bi, qi: (bi, 6)),
            pl.BlockSpec((s, 128), lambda bi, qi: (bi, 7)),
            _resident((1, A_HEAD_DIM)), _resident((1, A_KV_DIM)),
        ],
        out_specs=pl.BlockSpec((tq, A_WIDTH), lambda bi, qi: (bi * nq + qi, 0)),
        out_shape=jax.ShapeDtypeStruct((t, A_WIDTH), F32),
        scratch_shapes=[pltpu.VMEM((tq, s), jnp.int32), pltpu.VMEM((tq, s), F32)],
        compiler_params=_params("parallel", "arbitrary"), name="dsa",
    )(ua, ua, ua, ua, ua, q_gain, k_gain)


def _rwkv_prep_kernel(ub_ref, mu_ref, w0_ref, a0_ref, kk_ref, ka_ref, w2_ref, a2_ref, g2_ref,
                      r_o, lw_o, k_o, v_o, kk_o, a_o, g_o, carry_scr, *, tm, tiles_per_seq):
    i = pl.program_id(0)

    @pl.when(i % tiles_per_seq == 0)
    def _():
        carry_scr[...] = jnp.zeros_like(carry_scr)

    u = ub_ref[...]
    prev = pltpu.roll(u, 1, 0)
    prev = jnp.where(_iota((tm, 1), 0) == 0, carry_scr[...], prev)
    carry_scr[...] = u[tm - 1:tm, :]
    us = u + mu_ref[...] * (prev - u)

    w = B_WIDTH
    r = us[:, :w]
    k = us[:, w:2 * w]
    v = us[:, 2 * w:3 * w]
    o = 3 * w
    xw = us[:, o:o + DECAY_LORA]
    xa = us[:, o + DECAY_LORA:o + DECAY_LORA + AAA_LORA]
    xg = us[:, o + DECAY_LORA + AAA_LORA:]

    w_raw = -_softplus(-(w0_ref[...] + _dot(jnp.tanh(xw), w2_ref[...]))) - 0.5
    a = _sigmoid(a0_ref[...] + _dot(xa, a2_ref[...]))
    g = _dot(_sigmoid(xg), g2_ref[...])
    kk = k * kk_ref[...]
    nrm2 = _dot_exact_rhs(kk * kk, _group_ones(w, B_HEAD_DIM))
    kk = kk / jnp.maximum(jnp.sqrt(nrm2), 1e-12)

    r_o[...] = r
    lw_o[...] = -jnp.exp(w_raw)
    k_o[...] = k * (1.0 + (a - 1.0) * ka_ref[...])
    v_o[...] = v
    kk_o[...] = kk
    a_o[...] = a
    g_o[...] = g


def _rwkv_prep(ub, mu, w0, a0, k_k, k_a, w2, a2, g2, s, tm):
    t = ub.shape[0]
    kern = functools.partial(_rwkv_prep_kernel, tm=tm, tiles_per_seq=s // tm)
    row = pl.BlockSpec((tm, B_WIDTH), lambda i: (i, 0))
    vec = _resident((1, B_WIDTH))
    return pl.pallas_call(
        kern,
        grid=(t // tm,),
        in_specs=[pl.BlockSpec((tm, B_TOTAL), lambda i: (i, 0)), _resident((1, B_TOTAL)),
                  vec, vec, vec, vec, _resident(w2.shape), _resident(a2.shape), _resident(g2.shape)],
        out_specs=[row] * 7,
        out_shape=[jax.ShapeDtypeStruct((t, B_WIDTH), F32)] * 7,
        scratch_shapes=[pltpu.VMEM((1, B_TOTAL), F32)],
        compiler_params=_params("arbitrary"), name="rwkv_prep",
    )(ub, mu, w0, a0, k_k, k_a, w2, a2, g2)


def _rwkv_chunk_kernel(r_ref, lw_ref, k_ref, v_ref, kk_ref, a_ref, g_ref, rk_ref, gng_ref, gnb_ref,
                       out_ref, state_scr, y_scr, *, lc):
    c = pl.program_id(1)

    @pl.when(c == 0)
    def _():
        state_scr[...] = jnp.zeros_like(state_scr)

    n = B_HEAD_DIM
    lw = lw_ref[...]
    rows = _iota((lc, lc), 0)
    cols = _iota((lc, lc), 1)
    incl = rows >= cols
    strict = rows > cols
    cum = _dot_exact_lhs(incl.astype(F32), lw)
    gam = jnp.exp(cum)
    gam_prev = jnp.exp(cum - lw)
    gam_inv = jnp.exp(-cum)
    gam_last = gam[lc - 1:lc, :]
    kk = kk_ref[...]
    v_all = v_ref[...]
    at = (-kk * gam_prev).astype(BF16)
    bt_f = kk * a_ref[...] * gam_inv
    kt_f = k_ref[...] * gam_inv
    bt = bt_f.astype(BF16)
    kt = kt_f.astype(BF16)
    rt = (r_ref[...] * gam).astype(BF16)
    be = (bt_f * gam_last).astype(BF16)
    ke = (kt_f * gam_last).astype(BF16)
    vb = v_all.astype(BF16)
    eye = (rows == cols).astype(F32)

    for h in range(B_HEADS):
        sl = slice(h * n, (h + 1) * n)
        at_h, rt_h, bt_h, kt_h, v_h = at[:, sl], rt[:, sl], bt[:, sl], kt[:, sl], vb[:, sl]
        ar = jnp.concatenate([at_h, rt_h], axis=0)
        pb = _dot_nt(ar, bt_h)
        pk = _dot_nt(ar, kt_h)
        a_ab = jnp.where(strict, pb[:lc], 0.0)
        a_ak = jnp.where(strict, pk[:lc], 0.0)
        a_qb = jnp.where(incl, pb[lc:], 0.0)
        a_qk = jnp.where(incl, pk[lc:], 0.0)
        x = eye + a_ab
        pw = a_ab
        span = 2
        while span < lc:
            pw = _dot(pw, pw)
            x = x + _dot(x, pw)
            span *= 2
        w_mat = _dot(x, at_h)
        u_v = _dot(x, _dot(a_ak, v_h))
        y_intra = _dot(a_qk, v_h)
        kv_new = _dot_tn(v_h, ke[:, sl])
        s0 = state_scr[h]
        u = _dot_nt(w_mat, s0) + u_v
        y_scr[:, sl] = _dot_nt(rt_h, s0) + _dot(a_qb, u) + y_intra
        state_scr[h] = s0 * gam_last[:, sl] + _dot_tn(u, be[:, sl]) + kv_new

    y = y_scr[...]
    ones = _group_ones(B_WIDTH, n)
    mean = _dot_exact_rhs(y, ones) * (1.0 / n)
    yc = y - mean
    var = _dot_exact_rhs(yc * yc, ones) * (1.0 / n)
    yn = yc * lax.rsqrt(var + RWKV_GN_EPS) * gng_ref[...] + gnb_ref[...]
    bonus = _dot_exact_rhs(r_ref[...] * k_ref[...] * rk_ref[...], ones) * v_all
    out_ref[...] = (yn + bonus) * g_ref[...]


def _rwkv_chunks(r, lw, k, v, kk, a, g, r_k, gn_g, gn_b, b, s, lc):
    t = b * s
    nc = s // lc
    kern = functools.partial(_rwkv_chunk_kernel, lc=lc)
    row = pl.BlockSpec((lc, B_WIDTH), lambda bi, ci: (bi * nc + ci, 0))
    vec = _resident((1, B_WIDTH))
    return pl.pallas_call(
        kern,
        grid=(b, nc),
        in_specs=[row] * 7 + [vec] * 3,
        out_specs=row,
        out_shape=jax.ShapeDtypeStruct((t, B_WIDTH), F32),
        scratch_shapes=[pltpu.VMEM((B_HEADS, B_HEAD_DIM, B_HEAD_DIM), F32), pltpu.VMEM((lc, B_WIDTH), F32)],
        compiler_params=_params("parallel", "arbitrary"), name="rwkv_chunks",
    )(r, lw, k, v, kk, a, g, r_k, gn_g, gn_b)


def _outproj_kernel(x_ref, ya_ref, yb_ref, wa_ref, wb_ref, o_ref):
    o_ref[...] = (x_ref[...] + jnp.dot(ya_ref[...].astype(BF16), wa_ref[...], preferred_element_type=F32)
                  + jnp.dot(yb_ref[...].astype(BF16), wb_ref[...], preferred_element_type=F32))


def _outproj(x, ya, yb, wa, wb, tm):
    t = x.shape[0]
    return pl.pallas_call(
        _outproj_kernel,
        grid=(t // tm,),
        in_specs=[pl.BlockSpec((tm, D_MODEL), lambda i: (i, 0)), pl.BlockSpec((tm, A_WIDTH), lambda i: (i, 0)),
                  pl.BlockSpec((tm, B_WIDTH), lambda i: (i, 0)), _resident(wa.shape), _resident(wb.shape)],
        out_specs=pl.BlockSpec((tm, D_MODEL), lambda i: (i, 0)),
        out_shape=jax.ShapeDtypeStruct((t, D_MODEL), F32),
        compiler_params=_params("parallel"), name="outproj",
    )(x, ya, yb, wa, wb)


def _ffn_kernel(x_ref, g_ref, wg_ref, wu_ref, wd_ref, o_ref, *, th):
    x = x_ref[...]
    h = _rms(x, g_ref[...]).astype(BF16)
    acc = x
    for j in range(FFN_HIDDEN // th):
        cols = slice(j * th, (j + 1) * th)
        gate = jnp.dot(h, wg_ref[:, cols], preferred_element_type=F32)
        up = jnp.dot(h, wu_ref[:, cols], preferred_element_type=F32)
        acc = acc + jnp.dot((_silu(gate) * up).astype(BF16), wd_ref[cols, :], preferred_element_type=F32)
    o_ref[...] = acc


def _ffn(x, g, wg, wu, wd, tm, th):
    t = x.shape[0]
    return pl.pallas_call(
        functools.partial(_ffn_kernel, th=th),
        grid=(t // tm,),
        in_specs=[pl.BlockSpec((tm, D_MODEL), lambda i: (i, 0)), _resident((1, D_MODEL)),
                  _resident(wg.shape), _resident(wu.shape), _resident(wd.shape)],
        out_specs=pl.BlockSpec((tm, D_MODEL), lambda i: (i, 0)),
        out_shape=jax.ShapeDtypeStruct((t, D_MODEL), F32),
        compiler_params=_params("parallel"), name="ffn",
    )(x, g, wg, wu, wd)


def _ple_kernel(x_ref, p_ref, g_ref, wp_ref, wg_ref, o_ref):
    x = x_ref[...]
    gate = _sigmoid(jnp.dot(_rms(x, g_ref[...]).astype(BF16), wg_ref[...], preferred_element_type=F32))
    o_ref[...] = x + jnp.dot(p_ref[...].astype(BF16), wp_ref[...], preferred_element_type=F32) * gate


def _ple(x, p, g, wp, wg, tm):
    t = x.shape[0]
    return pl.pallas_call(
        _ple_kernel,
        grid=(t // tm,),
        in_specs=[pl.BlockSpec((tm, D_MODEL), lambda i: (i, 0)), pl.BlockSpec((tm, PLE_DIM), lambda i: (i, 0)),
                  _resident((1, D_MODEL)), _resident(wp.shape), _resident(wg.shape)],
        out_specs=pl.BlockSpec((tm, D_MODEL), lambda i: (i, 0)),
        out_shape=jax.ShapeDtypeStruct((t, D_MODEL), F32),
        compiler_params=_params("parallel"), name="ple",
    )(x, p, g, wp, wg)


def _upproj_kernel(x_ref, g_ref, w_ref, xm_ref, z_ref):
    h = _rms(x_ref[...], g_ref[...]).astype(BF16)
    xm_ref[...] = jnp.dot(h, w_ref[:, :C_INNER], preferred_element_type=F32)
    z_ref[...] = jnp.dot(h, w_ref[:, C_INNER:], preferred_element_type=F32)


def _upproj(x, g, w, tm):
    t = x.shape[0]
    return pl.pallas_call(
        _upproj_kernel,
        grid=(t // tm,),
        in_specs=[pl.BlockSpec((tm, D_MODEL), lambda i: (i, 0)), _resident((1, D_MODEL)), _resident(w.shape)],
        out_specs=[pl.BlockSpec((tm, C_INNER), lambda i: (i, 0))] * 2,
        out_shape=[jax.ShapeDtypeStruct((t, C_INNER), F32)] * 2,
        compiler_params=_params("parallel"), name="upproj",
    )(x, g, w)


def _conv_qkv_kernel(xm_ref, cw_ref, cb_ref, wq_ref, wk_ref, wv_ref, wif_ref, bif_ref,
                     xc_o, q_o, k_o, v_o, gcol_o, grow_o, carry_scr, *, tm, tiles_per_seq):
    i = pl.program_id(0)

    @pl.when(i % tiles_per_seq == 0)
    def _():
        carry_scr[...] = jnp.zeros_like(carry_scr)

    xm = xm_ref[...]
    carry = carry_scr[...]
    row8 = _iota((8, 1), 0)
    acc = cb_ref[...] + cw_ref[C_CONV - 1:C_CONV, :] * xm
    for sft in range(1, C_CONV):
        rolled = pltpu.roll(xm, sft, 0)
        head = jnp.where(row8 < sft, pltpu.roll(carry, sft, 0), rolled[:8])
        shifted = jnp.concatenate([head, rolled[8:]], axis=0)
        acc = acc + cw_ref[C_CONV - 1 - sft:C_CONV - sft, :] * shifted
    carry_scr[...] = xm[tm - 8:, :]
    xc = _silu(acc)
    xc_o[...] = xc

    xcb = xc.astype(BF16)
    xmb = xm.astype(BF16)
    gates = jnp.zeros((tm, GATE_PAD), F32)
    for j in range(C_INNER // QKV_TILE):
        cols = slice(j * QKV_TILE, (j + 1) * QKV_TILE)
        qj = jnp.dot(xcb[:, cols], wq_ref[j], preferred_element_type=F32)
        kj = jnp.dot(xcb[:, cols], wk_ref[j], preferred_element_type=F32)
        vj = jnp.dot(xmb[:, cols], wv_ref[j], preferred_element_type=F32)
        q_o[:, cols] = qj
        k_o[:, cols] = kj
        v_o[:, cols] = vj
        gates = gates + (jnp.dot(qj.astype(BF16), wif_ref[0, cols, :], preferred_element_type=F32)
                         + jnp.dot(kj.astype(BF16), wif_ref[1, cols, :], preferred_element_type=F32)
                         + jnp.dot(vj.astype(BF16), wif_ref[2, cols, :], preferred_element_type=F32))
    gates = gates + bif_ref[...]
    lane = _iota((tm, GATE_PAD), 1)
    gates = jnp.where(lane < C_HEADS, gates, -_softplus(-gates))
    gcol_o[...] = gates[:, :2 * C_HEADS]
    grow_o[...] = gates.T[:2 * C_HEADS, :]


def _conv_qkv(xm, cw, cb, wq, wk, wv, wif, bif, s, tm):
    t = xm.shape[0]
    kern = functools.partial(_conv_qkv_kernel, tm=tm, tiles_per_seq=s // tm)
    row = pl.BlockSpec((tm, C_INNER), lambda i: (i, 0))
    return pl.pallas_call(
        kern,
        grid=(t // tm,),
        in_specs=[row, _resident(cw.shape), _resident(cb.shape), _resident(wq.shape), _resident(wk.shape),
                  _resident(wv.shape), _resident(wif.shape), _resident(bif.shape)],
        out_specs=[row, row, row, row, pl.BlockSpec((tm, 2 * C_HEADS), lambda i: (i, 0)),
                   pl.BlockSpec((2 * C_HEADS, tm), lambda i: (0, i))],
        out_shape=[jax.ShapeDtypeStruct((t, C_INNER), F32)] * 4
        + [jax.ShapeDtypeStruct((t, 2 * C_HEADS), F32), jax.ShapeDtypeStruct((2 * C_HEADS, t), F32)],
        scratch_shapes=[pltpu.VMEM((8, C_INNER), F32)],
        compiler_params=_params("arbitrary"), name="conv_qkv",
    )(xm, cw, cb, wq, wk, wv, wif, bif)


def _mlstm_kernel(q_ref, k_ref, v_ref, gcol_ref, grow_ref, h_ref, c_scr, n_scr, m_scr, *, lc):
    c = pl.program_id(1)

    @pl.when(c == 0)
    def _():
        c_scr[...] = jnp.zeros_like(c_scr)
        n_scr[...] = jnp.zeros_like(n_scr)
        m_scr[...] = jnp.zeros_like(m_scr)

    d_h = C_HEAD_DIM
    rows = _iota((lc, lc), 0)
    cols = _iota((lc, lc), 1)
    tril = rows >= cols
    gcol = gcol_ref[...]
    grow = grow_ref[...]
    bcol = _dot_exact_lhs(tril.astype(F32), gcol)
    brow = _dot_exact_rhs(grow, (rows <= cols).astype(F32))

    for h in range(C_HEADS):
        hs = slice(h * d_h, (h + 1) * d_h)
        b_c = bcol[:, C_HEADS + h:C_HEADS + h + 1]
        b_r = brow[C_HEADS + h:C_HEADS + h + 1, :]
        i_c = gcol[:, h:h + 1]
        i_r = grow[h:h + 1, :]
        m_prev = m_scr[h][0:1, 0:1]

        dmat = jnp.where(tril, b_c - b_r + i_r, -jnp.inf)
        inter = b_c + m_prev
        m_t = jnp.maximum(inter, jnp.max(dmat, axis=-1, keepdims=True))
        wts = jnp.exp(dmat - m_t)
        sc = jnp.exp(inter - m_t)

        qh = q_ref[:, hs]
        kh = k_ref[:, hs] * (d_h ** -0.5)
        vh = v_ref[:, hs].astype(BF16)
        qb = qh.astype(BF16)
        qk = _dot_nt(qb, kh) * wts
        c_prev = c_scr[h]
        n_prev = n_scr[h][0:1, :]
        num = jnp.dot(qk.astype(BF16), vh, preferred_element_type=F32) + sc * _dot(qb, c_prev)
        den = jnp.sum(qk, axis=-1, keepdims=True) + sc * jnp.sum(qh * n_prev, axis=-1, keepdims=True)
        h_ref[:, hs] = num / jnp.maximum(jnp.abs(den), jnp.exp(-m_t))

        b_last = b_r[:, lc - 1:lc]
        m_new = jnp.maximum(b_last + m_prev, jnp.max(b_last - b_r + i_r, axis=-1, keepdims=True))
        dc = jnp.exp(b_last + m_prev - m_new)
        kws = kh * jnp.exp(b_last - b_c + i_c - m_new)
        c_scr[h] = dc * c_prev + _dot_tn(kws, vh)
        n_scr[h] = jnp.broadcast_to(dc * n_prev + jnp.sum(kws, axis=0, keepdims=True), (8, d_h))
        m_scr[h] = jnp.broadcast_to(m_new, (8, 128))


def _mlstm(q, k, v, gcol, grow, b, s, lc):
    t = b * s
    nc = s // lc
    row = pl.BlockSpec((lc, C_INNER), lambda bi, ci: (bi * nc + ci, 0))
    return pl.pallas_call(
        functools.partial(_mlstm_kernel, lc=lc),
        grid=(b, nc),
        in_specs=[row, row, row, pl.BlockSpec((lc, 2 * C_HEADS), lambda bi, ci: (bi * nc + ci, 0)),
                  pl.BlockSpec((2 * C_HEADS, lc), lambda bi, ci: (0, bi * nc + ci))],
        out_specs=row,
        out_shape=jax.ShapeDtypeStruct((t, C_INNER), F32),
        scratch_shapes=[pltpu.VMEM((C_HEADS, C_HEAD_DIM, C_HEAD_DIM), F32),
                        pltpu.VMEM((C_HEADS, 8, C_HEAD_DIM), F32), pltpu.VMEM((C_HEADS, 8, 128), F32)],
        compiler_params=_params("parallel", "arbitrary"), name="mlstm",
    )(q, k, v, gcol, grow)


def _downproj_kernel(x_ref, h_ref, xc_ref, z_ref, mg_ref, sk_ref, w_ref, o_ref):
    hcat = []
    for h in range(C_HEADS):
        y = h_ref[:, h * C_HEAD_DIM:(h + 1) * C_HEAD_DIM]
        mean = jnp.mean(y, axis=-1, keepdims=True)
        yc = y - mean
        var = jnp.mean(yc * yc, axis=-1, keepdims=True)
        hcat.append(yc * lax.rsqrt(var + C_GN_EPS))
    hn = jnp.concatenate(hcat, axis=-1) * mg_ref[...]
    y = (hn + sk_ref[...] * xc_ref[...]) * _silu(z_ref[...])
    o_ref[...] = x_ref[...] + jnp.dot(y.astype(BF16), w_ref[...], preferred_element_type=F32)


def _downproj(x, h, xc, z, mh_g, skip, w, tm):
    t = x.shape[0]
    row = pl.BlockSpec((tm, C_INNER), lambda i: (i, 0))
    return pl.pallas_call(
        _downproj_kernel,
        grid=(t // tm,),
        in_specs=[pl.BlockSpec((tm, D_MODEL), lambda i: (i, 0)), row, row, row,
                  _resident((1, C_INNER)), _resident((1, C_INNER)), _resident(w.shape)],
        out_specs=pl.BlockSpec((tm, D_MODEL), lambda i: (i, 0)),
        out_shape=jax.ShapeDtypeStruct((t, D_MODEL), F32),
        compiler_params=_params("parallel"), name="downproj",
    )(x, h, xc, z, mh_g, skip, w)


def _pack_w_in(w_in):
    q, k, v, iq, ik, iw = jnp.split(w_in[:, :A_TOTAL], [512, 576, 640, 896, 960], axis=1)
    pad = jnp.zeros((D_MODEL, A_PAD - A_TOTAL), w_in.dtype)
    wa = jnp.concatenate([q, iq, k, v, ik, iw, pad], axis=1)
    return wa.astype(BF16), w_in[:, A_TOTAL:].astype(BF16)


def _block_diag_tiles(w):
    per = QKV_TILE // C_QKV_BLOCK
    wt = w.reshape(C_INNER // QKV_TILE, per, C_QKV_BLOCK, C_QKV_BLOCK)
    eye = jnp.eye(per, dtype=w.dtype)
    full = wt[:, :, :, None, :] * eye[None, :, None, :, None]
    return full.reshape(C_INNER // QKV_TILE, QKV_TILE, QKV_TILE).astype(BF16)


def _row_tile(s, want):
    return min(want, s)


def kernel(x, p, mix_norm, a_q_gain, a_k_gain, w_in_e, b_mu, b_w0, b_w2, b_a0, b_a2, b_g2, b_k_k, b_k_a,
           b_r_k, b_gn_g, b_gn_b, w_out_e, c_w_up, c_conv_w, c_conv_b, c_wq, c_wk, c_wv, c_w_if, c_b_i,
           c_b_f, c_mh_g, c_skip, c_w_down, ffn_norm, ffn_w_gate, ffn_w_up, ffn_w_down, ple_w, ple_norm,
           ple_w_gate):
    b, s, d = x.shape
    t = b * s
    depth = p.shape[0]
    tm = _row_tile(s, 512)
    xf = x.reshape(t, d)
    pf = p.reshape(depth, t, PLE_DIM)
    vec = lambda a: a.reshape(1, -1)

    for i in range(depth):
        j = i // 2
        if i % 2 == 0:
            wa, wb = _pack_w_in(w_in_e[j])
            ua, ub = _inproj(xf, vec(mix_norm[i]), wa, wb, tm)
            ya = _dsa(ua, vec(a_q_gain[j]), vec(a_k_gain[j]), b, s, _row_tile(s, 256))
            r, lw, k, v, kk, a, g = _rwkv_prep(
                ub, vec(b_mu[j]), vec(b_w0[j]), vec(b_a0[j]), vec(b_k_k[j]), vec(b_k_a[j]),
                b_w2[j].astype(BF16), b_a2[j].astype(BF16), b_g2[j].astype(BF16), s, tm)
            yb = _rwkv_chunks(r, lw, k, v, kk, a, g, vec(b_r_k[j]), vec(b_gn_g[j]), vec(b_gn_b[j]), b, s,
                              _row_tile(s, CHUNK))
            wo = w_out_e[j].astype(BF16)
            xf = _outproj(xf, ya, yb, wo[:A_WIDTH], wo[A_WIDTH:], tm)
        else:
            xm, z = _upproj(xf, vec(mix_norm[i]), c_w_up[j].astype(BF16), tm)
            wif = jnp.pad(c_w_if[j], ((0, 0), (0, GATE_PAD - 2 * C_HEADS))).reshape(3, C_INNER, GATE_PAD)
            bif = jnp.pad(jnp.concatenate([c_b_i[j], c_b_f[j]]), (0, GATE_PAD - 2 * C_HEADS)).reshape(1, GATE_PAD)
            xc, q, k, v, gcol, grow = _conv_qkv(
                xm, c_conv_w[j].reshape(C_CONV, C_INNER), vec(c_conv_b[j]), _block_diag_tiles(c_wq[j]),
                _block_diag_tiles(c_wk[j]), _block_diag_tiles(c_wv[j]), wif.astype(BF16), bif, s,
                _row_tile(s, 256))
            hcell = _mlstm(q, k, v, gcol, grow, b, s, _row_tile(s, 256))
            xf = _downproj(xf, hcell, xc, z, vec(c_mh_g[j]), vec(c_skip[j]), c_w_down[j].astype(BF16), tm)
        xf = _ffn(xf, vec(ffn_norm[i]), ffn_w_gate[i].astype(BF16), ffn_w_up[i].astype(BF16),
                  ffn_w_down[i].astype(BF16), tm, 256)
        xf = _ple(xf, pf[i], vec(ple_norm[i]), ple_w[i].astype(BF16), ple_w_gate[i].astype(BF16), tm)
    return xf.reshape(b, s, d)
```

```python
import functools
import math

import jax
import jax.numpy as jnp
from jax import lax
from jax.experimental import pallas as pl
from jax.experimental.pallas import tpu as pltpu

F32 = jnp.float32
BF16 = jnp.bfloat16

D_MODEL = 1024
CHUNK = 64
PLE_DIM = 256
NORM_EPS = 1e-6
A_HEADS = 8
A_HEAD_DIM = 64
A_KV_DIM = 64
A_WIDTH = A_HEADS * A_HEAD_DIM
IDX_HEADS = 4
IDX_DIM = 64
TOPK_MAX = 256
B_HEADS = 8
B_HEAD_DIM = 64
B_WIDTH = B_HEADS * B_HEAD_DIM
DECAY_LORA = 64
AAA_LORA = 64
GATE_LORA = 128
RWKV_GN_EPS = 64e-5
A_TOTAL = A_WIDTH + 2 * A_KV_DIM + IDX_HEADS * IDX_DIM + IDX_DIM + IDX_HEADS
B_TOTAL = 3 * B_WIDTH + DECAY_LORA + AAA_LORA + GATE_LORA
A_PAD = 1024
C_INNER = 2 * D_MODEL
C_HEADS = 4
C_HEAD_DIM = C_INNER // C_HEADS
C_CONV = 4
C_QKV_BLOCK = 4
C_GN_EPS = 1e-5
FFN_HIDDEN = 2816
MXU_TILE = 256
QKV_TILE = MXU_TILE
GATE_PAD = 128
RWKV_GROUP_HEADS = MXU_TILE // B_HEAD_DIM

VMEM_LIMIT = 56 * 1024 * 1024


def _params(*sem):
    return pltpu.CompilerParams(dimension_semantics=sem, vmem_limit_bytes=VMEM_LIMIT)


def _resident(shape):
    nd = len(shape)
    return pl.BlockSpec(shape, lambda *_: (0,) * nd, pipeline_mode=pl.Buffered(1))


def _rms(x, g):
    return x * lax.rsqrt(jnp.mean(x * x, axis=-1, keepdims=True) + NORM_EPS) * g


def _dot(a, b):
    return jnp.dot(a.astype(BF16), b.astype(BF16), preferred_element_type=F32)


def _dot_nt(a, b):
    return lax.dot_general(a.astype(BF16), b.astype(BF16), (((1,), (1,)), ((), ())),
                           preferred_element_type=F32)


def _dot_tn(a, b):
    return lax.dot_general(a.astype(BF16), b.astype(BF16), (((0,), (0,)), ((), ())),
                           preferred_element_type=F32)


def _split3(x):
    hi = x.astype(BF16)
    r1 = x - hi.astype(F32)
    mid = r1.astype(BF16)
    lo = (r1 - mid.astype(F32)).astype(BF16)
    return hi, mid, lo


def _dot_exact_lhs(a_exact, x):
    a = a_exact.astype(BF16)
    hi, mid, lo = _split3(x)
    return (jnp.dot(a, hi, preferred_element_type=F32) + jnp.dot(a, mid, preferred_element_type=F32)
            + jnp.dot(a, lo, preferred_element_type=F32))


def _dot_exact_rhs(x, b_exact):
    b = b_exact.astype(BF16)
    hi, mid, lo = _split3(x)
    return (jnp.dot(hi, b, preferred_element_type=F32) + jnp.dot(mid, b, preferred_element_type=F32)
            + jnp.dot(lo, b, preferred_element_type=F32))


def _group_sum(x, ones):
    hi = x.astype(BF16)
    lo = (x - hi.astype(F32)).astype(BF16)
    return jnp.dot(hi, ones, preferred_element_type=F32) + jnp.dot(lo, ones, preferred_element_type=F32)


def _sigmoid(x):
    return 1.0 / (1.0 + jnp.exp(-x))


def _softplus(x):
    return jnp.maximum(x, 0.0) + jnp.log(1.0 + jnp.exp(-jnp.abs(x)))


def _silu(x):
    return x * _sigmoid(x)


def _iota(shape, dim):
    return lax.broadcasted_iota(jnp.int32, shape, dim)


def _log2(n):
    assert n & (n - 1) == 0
    return n.bit_length() - 1


def _same_group(shape, group):
    sh = _log2(group)
    return jnp.right_shift(_iota(shape, 0), sh) == jnp.right_shift(_iota(shape, 1), sh)


def _inproj_kernel(x_ref, g_ref, wa_ref, wb_ref, ua_ref, ub_ref):
    h = _rms(x_ref[...], g_ref[...]).astype(BF16)
    ua_ref[...] = jnp.dot(h, wa_ref[...], preferred_element_type=F32)
    ub_ref[...] = jnp.dot(h, wb_ref[...], preferred_element_type=F32)


def _inproj(x, g, wa, wb, tm):
    t = x.shape[0]
    return pl.pallas_call(
        _inproj_kernel,
        grid=(t // tm,),
        in_specs=[pl.BlockSpec((tm, D_MODEL), lambda i: (i, 0)), _resident((1, D_MODEL)),
                  _resident(wa.shape), _resident(wb.shape)],
        out_specs=[pl.BlockSpec((tm, A_PAD), lambda i: (i, 0)),
                   pl.BlockSpec((tm, B_TOTAL), lambda i: (i, 0))],
        out_shape=[jax.ShapeDtypeStruct((t, A_PAD), F32), jax.ShapeDtypeStruct((t, B_TOTAL), F32)],
        compiler_params=_params("parallel"), name="inproj",
    )(x, g, wa, wb)


def _select_topk(score, valid, key_scr, sel_scr, *, tq, s_eff, k_sel):
    lane = 128
    bits = pltpu.bitcast(score, jnp.int32)
    key_scr[:, :s_eff] = jnp.where(bits < 0, bits ^ jnp.int32(0x7FFFFFFF), bits)

    kf = jnp.float32(k_sel)

    def count_ge(cand):
        return jnp.sum(jnp.where(key_scr[:, :s_eff] >= cand, 1.0, 0.0), axis=-1, keepdims=True)

    int_min = jnp.int32(-2 ** 31)
    ans = jnp.where(count_ge(jnp.zeros((tq, 1), jnp.int32)) >= kf, jnp.int32(0), int_min)

    def bit_step(i, ans):
        cand = ans | jnp.left_shift(jnp.int32(1), jnp.int32(30) - i)
        return jnp.where(count_ge(cand) >= kf, cand, ans)

    thr = lax.fori_loop(0, 31, bit_step, ans)

    key = key_scr[:, :s_eff]
    gt = key > thr
    need = kf - jnp.sum(jnp.where(gt, 1.0, 0.0), axis=-1, keepdims=True)
    tri = (_iota((lane, lane), 0) <= _iota((lane, lane), 1)).astype(BF16)
    carry = jnp.zeros((tq, 1), F32)
    for j in range(s_eff // lane):
        cols = slice(j * lane, (j + 1) * lane)
        eq_j = key[:, cols] == thr
        rank = carry + jnp.dot(jnp.where(eq_j, 1.0, 0.0).astype(BF16), tri, preferred_element_type=F32)
        sel_j = (gt[:, cols] | (eq_j & (rank <= need))) & valid[:, cols]
        sel_scr[:, cols] = jnp.where(sel_j, 0.0, -jnp.inf)
        carry = rank[:, lane - 1:lane]


def _dsa_block(q_ref, iq_ref, iwq_ref, kv_ref, ikw_ref, qg_ref, kg_ref, out_ref, key_scr, sel_scr, o_scr,
               *, tq, row0, s_eff, k_sel):
    lane = 128
    nblk = s_eff // lane

    ik = ikw_ref[:s_eff, :IDX_DIM].astype(BF16)
    iw = iwq_ref[:, IDX_DIM:IDX_DIM + IDX_HEADS] * (IDX_HEADS ** -0.5 * IDX_DIM ** -0.5)
    iq = iq_ref[...]
    score = jnp.zeros((tq, s_eff), F32)
    for h in range(IDX_HEADS):
        lg = _dot_nt(iq[:, h * IDX_DIM:(h + 1) * IDX_DIM], ik)
        score = score + iw[:, h:h + 1] * jnp.maximum(lg, 0.0)
    row = _iota((tq, 1), 0) + row0
    limit = (jnp.right_shift(row, _log2(CHUNK)) + 1) * CHUNK
    valid = _iota((tq, s_eff), 1) < limit
    if s_eff <= k_sel:
        sel_scr[:, :s_eff] = jnp.where(valid, 0.0, -jnp.inf)
    else:
        _select_topk(jnp.where(valid, score + 0.0, -jnp.inf), valid, key_scr, sel_scr,
                     tq=tq, s_eff=s_eff, k_sel=k_sel)

    kv = kv_ref[:s_eff, :]
    kraw = kv[:, :A_KV_DIM]
    kn = (kraw * lax.rsqrt(jnp.mean(kraw * kraw, axis=-1, keepdims=True) + NORM_EPS) * kg_ref[...]).astype(BF16)
    v = kv[:, A_KV_DIM:2 * A_KV_DIM].astype(BF16)
    q = q_ref[...]
    neg = sel_scr[:, :s_eff]
    qscale = A_HEAD_DIM ** -0.5 * math.log2(math.e)
    for h in range(A_HEADS):
        qh = q[:, h * A_HEAD_DIM:(h + 1) * A_HEAD_DIM]
        qn = qh * lax.rsqrt(jnp.mean(qh * qh, axis=-1, keepdims=True) + NORM_EPS) * (qg_ref[...] * qscale)
        sc = _dot_nt(qn, kn) + neg
        p = jnp.exp2(sc - jnp.max(sc, axis=-1, keepdims=True))
        den = jnp.sum(p, axis=-1, keepdims=True)
        o_scr[:, h * A_HEAD_DIM:(h + 1) * A_HEAD_DIM] = jnp.dot(p.astype(BF16), v, preferred_element_type=F32) / den
    out_ref[...] = o_scr[...].astype(out_ref.dtype)


def _dsa_kernel(*refs, tq, s, k_sel):
    qi = pl.program_id(1)
    for qv in range(s // tq):
        @pl.when(qi == qv)
        def _(qv=qv):
            _dsa_block(*refs, tq=tq, row0=qv * tq, s_eff=(qv + 1) * tq, k_sel=k_sel)


def _dsa(ua, q_gain, k_gain, b, s, tq):
    t = b * s
    nq = s // tq
    k_sel = min(TOPK_MAX, s // 4)
    kern = functools.partial(_dsa_kernel, tq=tq, s=s, k_sel=k_sel)
    return pl.pallas_call(
        kern,
        grid=(b, nq),
        in_specs=[
            pl.BlockSpec((tq, A_WIDTH), lambda bi, qi: (bi * nq + qi, 0)),
            pl.BlockSpec((tq, 256), lambda bi, qi: (bi * nq + qi, 2)),
            pl.BlockSpec((tq, 128), lambda bi, qi: (bi * nq + qi, 7)),
            pl.BlockSpec((s, 128), lambda bi, qi: (bi, 6)),
            pl.BlockSpec((s, 128), lambda bi, qi: (bi, 7)),
            _resident((1, A_HEAD_DIM)), _resident((1, A_KV_DIM)),
        ],
        out_specs=pl.BlockSpec((tq, A_WIDTH), lambda bi, qi: (bi * nq + qi, 0)),
        out_shape=jax.ShapeDtypeStruct((t, A_WIDTH), BF16),
        scratch_shapes=[pltpu.VMEM((tq, s), jnp.int32), pltpu.VMEM((tq, s), F32), pltpu.VMEM((tq, A_WIDTH), F32)],
        compiler_params=_params("parallel", "arbitrary"), name="dsa",
    )(ua, ua, ua, ua, ua, q_gain, k_gain)


def _rwkv_kernel(ub_ref, mu_ref, w0_ref, a0_ref, kkw_ref, ka_ref, w2_ref, a2_ref, g2_ref, rk_ref, gng_ref,
                 gnb_ref, out_ref, state_scr, carry_scr, y_scr, *, lc, nsub):
    n = B_HEAD_DIM
    assert lc == n
    wd = B_WIDTH
    gw = MXU_TILE
    ngrp = wd // gw
    rows = lc * nsub
    c = pl.program_id(1)

    @pl.when(c == 0)
    def _():
        state_scr[...] = jnp.zeros_like(state_scr)
        carry_scr[...] = jnp.zeros_like(carry_scr)

    u = ub_ref[...]
    prev = pltpu.roll(u, 1, 0)
    prev = jnp.where(_iota((rows, 1), 0) == 0, carry_scr[...], prev)
    carry_scr[...] = u[rows - 1:rows, :]
    us = u + mu_ref[...] * (prev - u)
    r = us[:, :wd]
    k_in = us[:, wd:2 * wd]
    v = us[:, 2 * wd:3 * wd]
    o = 3 * wd
    xw = us[:, o:o + DECAY_LORA]
    xa = us[:, o + DECAY_LORA:o + DECAY_LORA + AAA_LORA]
    xg = us[:, o + DECAY_LORA + AAA_LORA:]
    w_raw = -_softplus(-(w0_ref[...] + _dot(jnp.tanh(xw), w2_ref[...]))) - 0.5
    lw = -jnp.exp(w_raw)
    a = _sigmoid(a0_ref[...] + _dot(xa, a2_ref[...]))
    g = _dot(_sigmoid(xg), g2_ref[...])
    ones_g = jnp.where(_same_group((gw, gw), n), 1.0, 0.0).astype(BF16)
    kk = k_in * kkw_ref[...]
    nrm2 = jnp.concatenate([_group_sum(kk[:, i * gw:(i + 1) * gw] ** 2, ones_g) for i in range(ngrp)], axis=1)
    kk = kk / jnp.maximum(jnp.sqrt(nrm2), 1e-12)
    k = k_in * (1.0 + (a - 1.0) * ka_ref[...])

    ri = _iota((rows, rows), 0)
    ci = _iota((rows, rows), 1)
    incl_rows = _same_group((rows, rows), lc) & (ri >= ci)
    cum = _dot_exact_lhs(jnp.where(incl_rows, 1.0, 0.0), lw)
    gam = jnp.exp(cum)
    gam_inv = jnp.exp(-cum)
    at = -kk * jnp.exp(cum - lw)
    bt = kk * a * gam_inv
    kt = k * gam_inv
    rt = r * gam

    bdmask = _same_group((gw, gw), n)
    lane_s = _iota((lc, gw), 1) & (n - 1)
    row_t = _iota((lc, gw), 0)
    strict = row_t > lane_s
    incl = row_t >= lane_s
    eye = jnp.where(row_t == lane_s, 1.0, 0.0)

    def bd(x):
        return jnp.where(bdmask, jnp.concatenate([x] * (gw // lc), axis=0), 0.0).astype(BF16)

    def stack(top, bottom):
        return jnp.concatenate([top, bottom], axis=0).astype(BF16)

    insts = [(j, i) for j in range(nsub) for i in range(ngrp)]

    def blk(x, j, i):
        return x[j * lc:(j + 1) * lc, i * gw:(i + 1) * gw]

    a_ab, a_ak, a_qb, a_qk = {}, {}, {}, {}
    for ji in insts:
        ar = stack(blk(at, *ji), blk(rt, *ji))
        pb = _dot_nt(ar, bd(blk(bt, *ji)))
        pk = _dot_nt(ar, bd(blk(kt, *ji)))
        a_ab[ji] = jnp.where(strict, pb[:lc], 0.0)
        a_ak[ji] = jnp.where(strict, pk[:lc], 0.0)
        a_qb[ji] = jnp.where(incl, pb[lc:], 0.0)
        a_qk[ji] = jnp.where(incl, pk[lc:], 0.0)

    x = {ji: eye + a_ab[ji] for ji in insts}
    pw = {ji: _dot(a_ab[ji], bd(a_ab[ji])) for ji in insts}
    for _ in range(_log2(lc) - 2):
        for ji in insts:
            res = jnp.dot(stack(x[ji], pw[ji]), bd(pw[ji]), preferred_element_type=F32)
            x[ji] = x[ji] + res[:lc]
            pw[ji] = res[lc:]
    for ji in insts:
        x[ji] = x[ji] + _dot(x[ji], bd(pw[ji]))

    w_mat, u_v, y_intra, kv_new, be, gl = {}, {}, {}, {}, {}, {}
    for ji in insts:
        j, i = ji
        t2 = jnp.dot(stack(a_ak[ji], a_qk[ji]), bd(blk(v, *ji)), preferred_element_type=F32)
        y_intra[ji] = t2[lc:]
        w_mat[ji] = _dot(x[ji], bd(blk(at, *ji)))
        u_v[ji] = _dot(x[ji], bd(t2[:lc]))
        gl[ji] = gam[(j + 1) * lc - 1:(j + 1) * lc, i * gw:(i + 1) * gw]
        be[ji] = blk(bt, *ji) * gl[ji]
        kv_new[ji] = jnp.where(bdmask, _dot_tn(blk(v, *ji), blk(kt, *ji) * gl[ji]), 0.0)

    state = [state_scr[i] for i in range(ngrp)]
    for ji in insts:
        j, i = ji
        s0 = state[i]
        r2 = _dot_nt(stack(w_mat[ji], blk(rt, *ji)), s0)
        uu = r2[:lc] + u_v[ji]
        y_scr[j * lc:(j + 1) * lc, i * gw:(i + 1) * gw] = r2[lc:] + _dot(a_qb[ji], bd(uu)) + y_intra[ji]
        state[i] = s0 * gl[ji] + jnp.where(bdmask, _dot_tn(uu, be[ji]), 0.0) + kv_new[ji]
    for i in range(ngrp):
        state_scr[i] = state[i]

    for i in range(ngrp):
        cols = slice(i * gw, (i + 1) * gw)
        y = y_scr[:, cols]
        mean = _group_sum(y, ones_g) * (1.0 / n)
        yc = y - mean
        var = _group_sum(yc * yc, ones_g) * (1.0 / n)
        yn = yc * lax.rsqrt(var + RWKV_GN_EPS) * gng_ref[:, cols] + gnb_ref[:, cols]
        bonus = _group_sum(r[:, cols] * k[:, cols] * rk_ref[:, cols], ones_g) * v[:, cols]
        out_ref[:, cols] = ((yn + bonus) * g[:, cols]).astype(out_ref.dtype)


def _rwkv(ub, mu, w0, a0, k_k, k_a, w2, a2, g2, r_k, gn_g, gn_b, b, s, lc, nsub):
    t = b * s
    rows = lc * nsub
    assert s % rows == 0
    nc = s // rows
    kern = functools.partial(_rwkv_kernel, lc=lc, nsub=nsub)
    vec = _resident((1, B_WIDTH))
    ngrp = B_WIDTH // MXU_TILE
    return pl.pallas_call(
        kern,
        grid=(b, nc),
        in_specs=[pl.BlockSpec((rows, B_TOTAL), lambda bi, ci: (bi * nc + ci, 0)), _resident((1, B_TOTAL)),
                  vec, vec, vec, vec, _resident(w2.shape), _resident(a2.shape), _resident(g2.shape),
                  vec, vec, vec],
        out_specs=pl.BlockSpec((rows, B_WIDTH), lambda bi, ci: (bi * nc + ci, 0)),
        out_shape=jax.ShapeDtypeStruct((t, B_WIDTH), BF16),
        scratch_shapes=[pltpu.VMEM((ngrp, MXU_TILE, MXU_TILE), F32), pltpu.VMEM((1, B_TOTAL), F32),
                        pltpu.VMEM((rows, B_WIDTH), F32)],
        compiler_params=_params("parallel", "arbitrary"), name="rwkv",
    )(ub, mu, w0, a0, k_k, k_a, w2, a2, g2, r_k, gn_g, gn_b)


def _outproj_kernel(x_ref, ya_ref, yb_ref, wa_ref, wb_ref, o_ref):
    o_ref[...] = (x_ref[...] + jnp.dot(ya_ref[...], wa_ref[...], preferred_element_type=F32)
                  + jnp.dot(yb_ref[...], wb_ref[...], preferred_element_type=F32))


def _outproj(x, ya, yb, wa, wb, tm):
    t = x.shape[0]
    return pl.pallas_call(
        _outproj_kernel,
        grid=(t // tm,),
        in_specs=[pl.BlockSpec((tm, D_MODEL), lambda i: (i, 0)), pl.BlockSpec((tm, A_WIDTH), lambda i: (i, 0)),
                  pl.BlockSpec((tm, B_WIDTH), lambda i: (i, 0)), _resident(wa.shape), _resident(wb.shape)],
        out_specs=pl.BlockSpec((tm, D_MODEL), lambda i: (i, 0)),
        out_shape=jax.ShapeDtypeStruct((t, D_MODEL), F32),
        compiler_params=_params("parallel"), name="outproj",
    )(x, ya, yb, wa, wb)


def _ffn_kernel(x_ref, g_ref, wg_ref, wu_ref, wd_ref, o_ref, *, th):
    x = x_ref[...]
    h = _rms(x, g_ref[...]).astype(BF16)
    acc = x
    for j in range(FFN_HIDDEN // th):
        cols = slice(j * th, (j + 1) * th)
        gate = jnp.dot(h, wg_ref[:, cols], preferred_element_type=F32)
        up = jnp.dot(h, wu_ref[:, cols], preferred_element_type=F32)
        acc = acc + jnp.dot((_silu(gate) * up).astype(BF16), wd_ref[cols, :], preferred_element_type=F32)
    o_ref[...] = acc


def _ffn(x, g, wg, wu, wd, tm, th):
    t = x.shape[0]
    return pl.pallas_call(
        functools.partial(_ffn_kernel, th=th),
        grid=(t // tm,),
        in_specs=[pl.BlockSpec((tm, D_MODEL), lambda i: (i, 0)), _resident((1, D_MODEL)),
                  _resident(wg.shape), _resident(wu.shape), _resident(wd.shape)],
        out_specs=pl.BlockSpec((tm, D_MODEL), lambda i: (i, 0)),
        out_shape=jax.ShapeDtypeStruct((t, D_MODEL), F32),
        compiler_params=_params("parallel"), name="ffn",
    )(x, g, wg, wu, wd)


def _ple_kernel(x_ref, p_ref, g_ref, wp_ref, wg_ref, o_ref):
    x = x_ref[...]
    gate = _sigmoid(jnp.dot(_rms(x, g_ref[...]).astype(BF16), wg_ref[...], preferred_element_type=F32))
    o_ref[...] = x + jnp.dot(p_ref[...].astype(BF16), wp_ref[...], preferred_element_type=F32) * gate


def _ple(x, p, g, wp, wg, tm):
    t = x.shape[0]
    return pl.pallas_call(
        _ple_kernel,
        grid=(t // tm,),
        in_specs=[pl.BlockSpec((tm, D_MODEL), lambda i: (i, 0)), pl.BlockSpec((tm, PLE_DIM), lambda i: (i, 0)),
                  _resident((1, D_MODEL)), _resident(wp.shape), _resident(wg.shape)],
        out_specs=pl.BlockSpec((tm, D_MODEL), lambda i: (i, 0)),
        out_shape=jax.ShapeDtypeStruct((t, D_MODEL), F32),
        compiler_params=_params("parallel"), name="ple",
    )(x, p, g, wp, wg)


def _upproj_kernel(x_ref, g_ref, w_ref, xm_ref, z_ref):
    h = _rms(x_ref[...], g_ref[...]).astype(BF16)
    xm_ref[...] = jnp.dot(h, w_ref[:, :C_INNER], preferred_element_type=F32).astype(xm_ref.dtype)
    z_ref[...] = jnp.dot(h, w_ref[:, C_INNER:], preferred_element_type=F32).astype(z_ref.dtype)


def _upproj(x, g, w, tm):
    t = x.shape[0]
    return pl.pallas_call(
        _upproj_kernel,
        grid=(t // tm,),
        in_specs=[pl.BlockSpec((tm, D_MODEL), lambda i: (i, 0)), _resident((1, D_MODEL)), _resident(w.shape)],
        out_specs=[pl.BlockSpec((tm, C_INNER), lambda i: (i, 0))] * 2,
        out_shape=[jax.ShapeDtypeStruct((t, C_INNER), BF16)] * 2,
        compiler_params=_params("parallel"), name="upproj",
    )(x, g, w)


def _conv_qkv_kernel(xm_ref, cw_ref, cb_ref, wq_ref, wk_ref, wv_ref, wif_ref, bif_ref,
                     xc_o, q_o, k_o, v_o, gcol_o, grow_o, carry_scr, *, tm, tiles_per_seq):
    i = pl.program_id(0)

    @pl.when(i % tiles_per_seq == 0)
    def _():
        carry_scr[...] = jnp.zeros_like(carry_scr)

    xmb = xm_ref[...]
    xm = xmb.astype(F32)
    carry = carry_scr[...]
    row8 = _iota((8, 1), 0)
    acc = cb_ref[...] + cw_ref[C_CONV - 1:C_CONV, :] * xm
    for sft in range(1, C_CONV):
        rolled = pltpu.roll(xm, sft, 0)
        head = jnp.where(row8 < sft, pltpu.roll(carry, sft, 0), rolled[:8])
        shifted = jnp.concatenate([head, rolled[8:]], axis=0)
        acc = acc + cw_ref[C_CONV - 1 - sft:C_CONV - sft, :] * shifted
    carry_scr[...] = xm[tm - 8:, :]
    xcb = _silu(acc).astype(BF16)
    xc_o[...] = xcb

    gates = jnp.zeros((tm, GATE_PAD), F32)
    for j in range(C_INNER // QKV_TILE):
        cols = slice(j * QKV_TILE, (j + 1) * QKV_TILE)
        qj = jnp.dot(xcb[:, cols], wq_ref[j], preferred_element_type=F32).astype(BF16)
        kj = jnp.dot(xcb[:, cols], wk_ref[j], preferred_element_type=F32).astype(BF16)
        vj = jnp.dot(xmb[:, cols], wv_ref[j], preferred_element_type=F32).astype(BF16)
        q_o[:, cols] = qj
        k_o[:, cols] = kj
        v_o[:, cols] = vj
        gates = gates + (jnp.dot(qj, wif_ref[0, cols, :], preferred_element_type=F32)
                         + jnp.dot(kj, wif_ref[1, cols, :], preferred_element_type=F32)
                         + jnp.dot(vj, wif_ref[2, cols, :], preferred_element_type=F32))
    gates = gates + bif_ref[...]
    lane = _iota((tm, GATE_PAD), 1)
    gates = jnp.where(lane < C_HEADS, gates, -_softplus(-gates))
    gcol_o[...] = gates[:, :2 * C_HEADS]
    grow_o[...] = gates.T[:2 * C_HEADS, :]


def _conv_qkv(xm, cw, cb, wq, wk, wv, wif, bif, s, tm):
    t = xm.shape[0]
    kern = functools.partial(_conv_qkv_kernel, tm=tm, tiles_per_seq=s // tm)
    row = pl.BlockSpec((tm, C_INNER), lambda i: (i, 0))
    return pl.pallas_call(
        kern,
        grid=(t // tm,),
        in_specs=[row, _resident(cw.shape), _resident(cb.shape), _resident(wq.shape), _resident(wk.shape),
                  _resident(wv.shape), _resident(wif.shape), _resident(bif.shape)],
        out_specs=[row, row, row, row, pl.BlockSpec((tm, 2 * C_HEADS), lambda i: (i, 0)),
                   pl.BlockSpec((2 * C_HEADS, tm), lambda i: (0, i))],
        out_shape=[jax.ShapeDtypeStruct((t, C_INNER), BF16)] * 4
        + [jax.ShapeDtypeStruct((t, 2 * C_HEADS), F32), jax.ShapeDtypeStruct((2 * C_HEADS, t), F32)],
        scratch_shapes=[pltpu.VMEM((8, C_INNER), F32)],
        compiler_params=_params("arbitrary"), name="conv_qkv",
    )(xm, cw, cb, wq, wk, wv, wif, bif)


def _mlstm_kernel(q_ref, k_ref, v_ref, gcol_ref, grow_ref, h_ref, c_scr, n_scr, m_scr, *, lc):
    c = pl.program_id(1)

    @pl.when(c == 0)
    def _():
        c_scr[...] = jnp.zeros_like(c_scr)
        n_scr[...] = jnp.zeros_like(n_scr)
        m_scr[...] = jnp.zeros_like(m_scr)

    d_h = C_HEAD_DIM
    rows = _iota((lc, lc), 0)
    cols = _iota((lc, lc), 1)
    tril = rows >= cols
    gcol = gcol_ref[...]
    grow = grow_ref[...]
    bcol = _dot_exact_lhs(jnp.where(tril, 1.0, 0.0), gcol)
    brow = _dot_exact_rhs(grow, jnp.where(rows <= cols, 1.0, 0.0))

    for h in range(C_HEADS):
        hs = slice(h * d_h, (h + 1) * d_h)
        b_c = bcol[:, C_HEADS + h:C_HEADS + h + 1]
        b_r = brow[C_HEADS + h:C_HEADS + h + 1, :]
        i_c = gcol[:, h:h + 1]
        i_r = grow[h:h + 1, :]
        m_prev = m_scr[h][0:1, 0:1]

        dmat = jnp.where(tril, b_c - b_r + i_r, -jnp.inf)
        inter = b_c + m_prev
        m_t = jnp.maximum(inter, jnp.max(dmat, axis=-1, keepdims=True))
        wts = jnp.exp(dmat - m_t)
        sc = jnp.exp(inter - m_t)

        qb = q_ref[:, hs]
        qh = qb.astype(F32)
        kh = k_ref[:, hs].astype(F32) * (d_h ** -0.5)
        vh = v_ref[:, hs]
        qk = _dot_nt(qb, kh) * wts
        c_prev = c_scr[h]
        n_prev = n_scr[h][0:1, :]
        num = jnp.dot(qk.astype(BF16), vh, preferred_element_type=F32) + sc * _dot(qb, c_prev)
        den = jnp.sum(qk, axis=-1, keepdims=True) + sc * jnp.sum(qh * n_prev, axis=-1, keepdims=True)
        h_ref[:, hs] = (num / jnp.maximum(jnp.abs(den), jnp.exp(-m_t))).astype(h_ref.dtype)

        b_last = b_r[:, lc - 1:lc]
        m_new = jnp.maximum(b_last + m_prev, jnp.max(b_last - b_r + i_r, axis=-1, keepdims=True))
        dc = jnp.exp(b_last + m_prev - m_new)
        kws = kh * jnp.exp(b_last - b_c + i_c - m_new)
        c_scr[h] = dc * c_prev + _dot_tn(kws, vh)
        n_scr[h] = jnp.broadcast_to(dc * n_prev + jnp.sum(kws, axis=0, keepdims=True), (8, d_h))
        m_scr[h] = jnp.broadcast_to(m_new, (8, 128))


def _mlstm(q, k, v, gcol, grow, b, s, lc):
    t = b * s
    nc = s // lc
    row = pl.BlockSpec((lc, C_INNER), lambda bi, ci: (bi * nc + ci, 0))
    return pl.pallas_call(
        functools.partial(_mlstm_kernel, lc=lc),
        grid=(b, nc),
        in_specs=[row, row, row, pl.BlockSpec((lc, 2 * C_HEADS), lambda bi, ci: (bi * nc + ci, 0)),
                  pl.BlockSpec((2 * C_HEADS, lc), lambda bi, ci: (0, bi * nc + ci))],
        out_specs=row,
        out_shape=jax.ShapeDtypeStruct((t, C_INNER), BF16),
        scratch_shapes=[pltpu.VMEM((C_HEADS, C_HEAD_DIM, C_HEAD_DIM), F32),
                        pltpu.VMEM((C_HEADS, 8, C_HEAD_DIM), F32), pltpu.VMEM((C_HEADS, 8, 128), F32)],
        compiler_params=_params("parallel", "arbitrary"), name="mlstm",
    )(q, k, v, gcol, grow)


def _downproj_kernel(x_ref, h_ref, xc_ref, z_ref, mg_ref, sk_ref, w_ref, o_ref):
    hcat = []
    for h in range(C_HEADS):
        y = h_ref[:, h * C_HEAD_DIM:(h + 1) * C_HEAD_DIM].astype(F32)
        mean = jnp.mean(y, axis=-1, keepdims=True)
        yc = y - mean
        var = jnp.mean(yc * yc, axis=-1, keepdims=True)
        hcat.append(yc * lax.rsqrt(var + C_GN_EPS))
    hn = jnp.concatenate(hcat, axis=-1) * mg_ref[...]
    y = (hn + sk_ref[...] * xc_ref[...].astype(F32)) * _silu(z_ref[...].astype(F32))
    o_ref[...] = x_ref[...] + jnp.dot(y.astype(BF16), w_ref[...], preferred_element_type=F32)


def _downproj(x, h, xc, z, mh_g, skip, w, tm):
    t = x.shape[0]
    row = pl.BlockSpec((tm, C_INNER), lambda i: (i, 0))
    return pl.pallas_call(
        _downproj_kernel,
        grid=(t // tm,),
        in_specs=[pl.BlockSpec((tm, D_MODEL), lambda i: (i, 0)), row, row, row,
                  _resident((1, C_INNER)), _resident((1, C_INNER)), _resident(w.shape)],
        out_specs=pl.BlockSpec((tm, D_MODEL), lambda i: (i, 0)),
        out_shape=jax.ShapeDtypeStruct((t, D_MODEL), F32),
        compiler_params=_params("parallel"), name="downproj",
    )(x, h, xc, z, mh_g, skip, w)


def _pack_w_in(w_in):
    q, k, v, iq, ik, iw = jnp.split(w_in[:, :A_TOTAL], [512, 576, 640, 896, 960], axis=1)
    pad = jnp.zeros((D_MODEL, A_PAD - A_TOTAL), w_in.dtype)
    wa = jnp.concatenate([q, iq, k, v, ik, iw, pad], axis=1)
    return wa.astype(BF16), w_in[:, A_TOTAL:].astype(BF16)


def _block_diag_tiles(w):
    per = QKV_TILE // C_QKV_BLOCK
    wt = w.reshape(C_INNER // QKV_TILE, per, C_QKV_BLOCK, C_QKV_BLOCK)
    eye = jnp.eye(per, dtype=w.dtype)
    full = wt[:, :, :, None, :] * eye[None, :, None, :, None]
    return full.reshape(C_INNER // QKV_TILE, QKV_TILE, QKV_TILE).astype(BF16)


def _row_tile(s, want):
    return min(want, s)


def kernel(x, p, mix_norm, a_q_gain, a_k_gain, w_in_e, b_mu, b_w0, b_w2, b_a0, b_a2, b_g2, b_k_k, b_k_a,
           b_r_k, b_gn_g, b_gn_b, w_out_e, c_w_up, c_conv_w, c_conv_b, c_wq, c_wk, c_wv, c_w_if, c_b_i,
           c_b_f, c_mh_g, c_skip, c_w_down, ffn_norm, ffn_w_gate, ffn_w_up, ffn_w_down, ple_w, ple_norm,
           ple_w_gate):
    b, s, d = x.shape
    t = b * s
    depth = p.shape[0]
    tm = _row_tile(s, 512)
    xf = x.reshape(t, d)
    pf = p.reshape(depth, t, PLE_DIM)
    vec = lambda a: a.reshape(1, -1)

    for i in range(depth):
        j = i // 2
        if i % 2 == 0:
            wa, wb = _pack_w_in(w_in_e[j])
            ua, ub = _inproj(xf, vec(mix_norm[i]), wa, wb, tm)
            ya = _dsa(ua, vec(a_q_gain[j]), vec(a_k_gain[j]), b, s, _row_tile(s, 256))
            yb = _rwkv(ub, vec(b_mu[j]), vec(b_w0[j]), vec(b_a0[j]), vec(b_k_k[j]), vec(b_k_a[j]),
                       b_w2[j].astype(BF16), b_a2[j].astype(BF16), b_g2[j].astype(BF16), vec(b_r_k[j]),
                       vec(b_gn_g[j]), vec(b_gn_b[j]), b, s, CHUNK, 4)
            wo = w_out_e[j].astype(BF16)
            xf = _outproj(xf, ya, yb, wo[:A_WIDTH], wo[A_WIDTH:], tm)
        else:
            xm, z = _upproj(xf, vec(mix_norm[i]), c_w_up[j].astype(BF16), tm)
            wif = jnp.pad(c_w_if[j], ((0, 0), (0, GATE_PAD - 2 * C_HEADS))).reshape(3, C_INNER, GATE_PAD)
            bif = jnp.pad(jnp.concatenate([c_b_i[j], c_b_f[j]]), (0, GATE_PAD - 2 * C_HEADS)).reshape(1, GATE_PAD)
            xc, q, k, v, gcol, grow = _conv_qkv(
                xm, c_conv_w[j].reshape(C_CONV, C_INNER), vec(c_conv_b[j]), _block_diag_tiles(c_wq[j]),
                _block_diag_tiles(c_wk[j]), _block_diag_tiles(c_wv[j]), wif.astype(BF16), bif, s,
                _row_tile(s, 256))
            hcell = _mlstm(q, k, v, gcol, grow, b, s, _row_tile(s, 256))
            xf = _downproj(xf, hcell, xc, z, vec(c_mh_g[j]), vec(c_skip[j]), c_w_down[j].astype(BF16), tm)
        xf = _ffn(xf, vec(ffn_norm[i]), ffn_w_gate[i].astype(BF16), ffn_w_up[i].astype(BF16),
                  ffn_w_down[i].astype(BF16), tm, 256)
        xf = _ple(xf, pf[i], vec(ple_norm[i]), ple_w[i].astype(BF16), ple_w_gate[i].astype(BF16), tm)
    return xf.reshape(b, s, d)
```

```python
import functools
import math

import jax
import jax.numpy as jnp
from jax import lax
from jax.experimental import pallas as pl
from jax.experimental.pallas import tpu as pltpu

F32 = jnp.float32
BF16 = jnp.bfloat16

D_MODEL = 1024
CHUNK = 64
PLE_DIM = 256
NORM_EPS = 1e-6
A_HEADS = 8
A_HEAD_DIM = 64
A_KV_DIM = 64
A_WIDTH = A_HEADS * A_HEAD_DIM
IDX_HEADS = 4
IDX_DIM = 64
TOPK_MAX = 256
B_HEADS = 8
B_HEAD_DIM = 64
B_WIDTH = B_HEADS * B_HEAD_DIM
DECAY_LORA = 64
AAA_LORA = 64
GATE_LORA = 128
RWKV_GN_EPS = 64e-5
A_TOTAL = A_WIDTH + 2 * A_KV_DIM + IDX_HEADS * IDX_DIM + IDX_DIM + IDX_HEADS
B_TOTAL = 3 * B_WIDTH + DECAY_LORA + AAA_LORA + GATE_LORA
A_PAD = 1024
C_INNER = 2 * D_MODEL
C_HEADS = 4
C_HEAD_DIM = C_INNER // C_HEADS
C_CONV = 4
C_QKV_BLOCK = 4
C_GN_EPS = 1e-5
FFN_HIDDEN = 2816
MXU_TILE = 256
QKV_TILE = MXU_TILE
GATE_PAD = 128
RWKV_GROUP_HEADS = MXU_TILE // B_HEAD_DIM

VMEM_LIMIT = 56 * 1024 * 1024


def _params(*sem):
    return pltpu.CompilerParams(dimension_semantics=sem, vmem_limit_bytes=VMEM_LIMIT)


def _resident(shape):
    nd = len(shape)
    return pl.BlockSpec(shape, lambda *_: (0,) * nd, pipeline_mode=pl.Buffered(1))


def _rms(x, g):
    return x * lax.rsqrt(jnp.mean(x * x, axis=-1, keepdims=True) + NORM_EPS) * g


def _dot(a, b):
    return jnp.dot(a.astype(BF16), b.astype(BF16), preferred_element_type=F32)


def _dot_nt(a, b):
    return lax.dot_general(a.astype(BF16), b.astype(BF16), (((1,), (1,)), ((), ())),
                           preferred_element_type=F32)


def _dot_tn(a, b):
    return lax.dot_general(a.astype(BF16), b.astype(BF16), (((0,), (0,)), ((), ())),
                           preferred_element_type=F32)


def _split3(x):
    hi = x.astype(BF16)
    r1 = x - hi.astype(F32)
    mid = r1.astype(BF16)
    lo = (r1 - mid.astype(F32)).astype(BF16)
    return hi, mid, lo


def _dot_exact_lhs(a_exact, x):
    a = a_exact.astype(BF16)
    hi, mid, lo = _split3(x)
    return (jnp.dot(a, hi, preferred_element_type=F32) + jnp.dot(a, mid, preferred_element_type=F32)
            + jnp.dot(a, lo, preferred_element_type=F32))


def _dot_exact_rhs(x, b_exact):
    b = b_exact.astype(BF16)
    hi, mid, lo = _split3(x)
    return (jnp.dot(hi, b, preferred_element_type=F32) + jnp.dot(mid, b, preferred_element_type=F32)
            + jnp.dot(lo, b, preferred_element_type=F32))


def _group_sum(x, ones):
    hi = x.astype(BF16)
    lo = (x - hi.astype(F32)).astype(BF16)
    return jnp.dot(hi, ones, preferred_element_type=F32) + jnp.dot(lo, ones, preferred_element_type=F32)


def _sigmoid(x):
    return 1.0 / (1.0 + jnp.exp(-x))


def _softplus(x):
    return jnp.maximum(x, 0.0) + jnp.log(1.0 + jnp.exp(-jnp.abs(x)))


def _silu(x):
    return x * _sigmoid(x)


def _iota(shape, dim):
    return lax.broadcasted_iota(jnp.int32, shape, dim)


def _log2(n):
    assert n & (n - 1) == 0
    return n.bit_length() - 1


def _same_group(shape, group):
    sh = _log2(group)
    return jnp.right_shift(_iota(shape, 0), sh) == jnp.right_shift(_iota(shape, 1), sh)


def _inproj_kernel(x_ref, g_ref, wa_ref, wb_ref, ua_ref, ub_ref):
    h = _rms(x_ref[...], g_ref[...]).astype(BF16)
    ua_ref[...] = jnp.dot(h, wa_ref[...], preferred_element_type=F32)
    ub_ref[...] = jnp.dot(h, wb_ref[...], preferred_element_type=F32)


def _inproj(x, g, wa, wb, tm):
    t = x.shape[0]
    return pl.pallas_call(
        _inproj_kernel,
        grid=(t // tm,),
        in_specs=[pl.BlockSpec((tm, D_MODEL), lambda i: (i, 0)), _resident((1, D_MODEL)),
                  _resident(wa.shape), _resident(wb.shape)],
        out_specs=[pl.BlockSpec((tm, A_PAD), lambda i: (i, 0)),
                   pl.BlockSpec((tm, B_TOTAL), lambda i: (i, 0))],
        out_shape=[jax.ShapeDtypeStruct((t, A_PAD), F32), jax.ShapeDtypeStruct((t, B_TOTAL), F32)],
        compiler_params=_params("parallel"), name="inproj",
    )(x, g, wa, wb)


def _dsa_kernel(q_ref, iq_ref, iwq_ref, kv_ref, ikw_ref, qg_ref, kg_ref, out_ref, key_scr, sel_scr, kn_scr,
                v_scr, *, tq, k_sel):
    tk = tq
    qi = pl.program_id(1)
    nkt = qi + 1
    kf = jnp.float32(k_sel)
    needs_selection = (qi + 1) * tq > k_sel

    @pl.when(qi == 0)
    def _():
        kv = kv_ref[...]
        kraw = kv[:, :A_KV_DIM]
        kn_scr[...] = (kraw * lax.rsqrt(jnp.mean(kraw * kraw, axis=-1, keepdims=True) + NORM_EPS)
                       * kg_ref[...]).astype(BF16)
        v_scr[...] = kv[:, A_KV_DIM:2 * A_KV_DIM].astype(BF16)

    qpos = qi * tq + _iota((1, tq), 1)
    limit = (jnp.right_shift(qpos, _log2(CHUNK)) + 1) * CHUNK

    def valid_tile(kt):
        return kt * tk + _iota((tk, tq), 0) < limit

    def tile_rows(ref, kt):
        return ref[pl.ds(pl.multiple_of(kt * tk, tk), tk), :]

    @pl.when(needs_selection)
    def _():
        iq = iq_ref[...].astype(BF16)
        iq_h = [iq[:, h * IDX_DIM:(h + 1) * IDX_DIM] for h in range(IDX_HEADS)]
        pick = jnp.where((_iota((8, 128), 1) == _iota((8, 128), 0) + IDX_DIM) & (_iota((8, 128), 0) < IDX_HEADS),
                         1.0, 0.0).astype(BF16)
        iw_t = sum(lax.dot_general(pick, part, (((1,), (1,)), ((), ())), preferred_element_type=F32)
                   for part in _split3(iwq_ref[...])) * (IDX_HEADS ** -0.5 * IDX_DIM ** -0.5)

        def score_body(kt, _):
            ik_t = tile_rows(ikw_ref, kt)[:, :IDX_DIM].astype(BF16)
            sc = jnp.zeros((tk, tq), F32)
            for h in range(IDX_HEADS):
                sc = sc + iw_t[h:h + 1, :] * jnp.maximum(_dot_nt(ik_t, iq_h[h]), 0.0)
            sc = jnp.where(valid_tile(kt), sc + 0.0, -jnp.inf)
            bits = pltpu.bitcast(sc, jnp.int32)
            key_scr[kt] = jnp.where(bits < 0, bits ^ jnp.int32(0x7FFFFFFF), bits)
            return 0

        lax.fori_loop(0, nkt, score_body, 0)

        def count_ge(cand):
            def body(kt, acc):
                hit = jnp.where(key_scr[kt] >= cand, jnp.float32(1.0), jnp.float32(0.0))
                return acc + jnp.sum(hit.reshape(tk // 8, 8, tq), axis=0)
            return jnp.sum(lax.fori_loop(0, nkt, body, jnp.zeros((8, tq), F32)), axis=0, keepdims=True)

        ans = jnp.where(count_ge(jnp.zeros((1, tq), jnp.int32)) >= kf, jnp.int32(0), jnp.int32(-2 ** 31))

        def bit_step(i, ans):
            cand = ans | jnp.left_shift(jnp.int32(1), jnp.int32(30) - i)
            return jnp.where(count_ge(cand) >= kf, cand, ans)

        thr = lax.fori_loop(0, 31, bit_step, ans)
        need = kf - count_ge(thr + 1)
        tril = jnp.where(_iota((tk, tk), 0) >= _iota((tk, tk), 1), 1.0, 0.0).astype(BF16)

        def tie_body(kt, carry):
            key = key_scr[kt]
            eq = key == thr
            rank = carry + jnp.dot(tril, jnp.where(eq, 1.0, 0.0).astype(BF16), preferred_element_type=F32)
            sel = ((key > thr) | (eq & (rank <= need))) & valid_tile(kt)
            sel_scr[kt] = jnp.where(sel, 0.0, -jnp.inf)
            return rank[tk - 1:tk, :]

        lax.fori_loop(0, nkt, tie_body, jnp.zeros((1, tq), F32))

    @pl.when(jnp.logical_not(needs_selection))
    def _():
        def body(kt, _):
            sel_scr[kt] = jnp.where(valid_tile(kt), 0.0, -jnp.inf)
            return 0
        lax.fori_loop(0, nkt, body, 0)

    q = q_ref[...]
    qscale = A_HEAD_DIM ** -0.5 * math.log2(math.e)
    qn = []
    for h in range(A_HEADS):
        qh = q[:, h * A_HEAD_DIM:(h + 1) * A_HEAD_DIM]
        qn.append((qh * lax.rsqrt(jnp.mean(qh * qh, axis=-1, keepdims=True) + NORM_EPS)
                   * (qg_ref[...] * qscale)).astype(BF16))

    def att_body(kt, carry):
        ms, ls, accs = carry
        kn_t = tile_rows(kn_scr, kt)
        v_t = tile_rows(v_scr, kt)
        neg = sel_scr[kt]
        sc = [_dot_nt(kn_t, qn[h]) + neg for h in range(A_HEADS)]
        m_new = [jnp.maximum(ms[h], jnp.max(sc[h], axis=0, keepdims=True)) for h in range(A_HEADS)]
        p = [jnp.exp2(sc[h] - m_new[h]) for h in range(A_HEADS)]
        alpha = [jnp.exp2(ms[h] - m_new[h]) for h in range(A_HEADS)]
        ls = [alpha[h] * ls[h] + jnp.sum(p[h], axis=0, keepdims=True) for h in range(A_HEADS)]
        accs = [alpha[h] * accs[h] + _dot_tn(v_t, p[h]) for h in range(A_HEADS)]
        return m_new, ls, accs

    init = ([jnp.full((1, tq), -1e30, F32)] * A_HEADS, [jnp.zeros((1, tq), F32)] * A_HEADS,
            [jnp.zeros((A_KV_DIM, tq), F32)] * A_HEADS)
    _, ls, accs = lax.fori_loop(0, nkt, att_body, init)
    out_t = jnp.concatenate([accs[h] / ls[h] for h in range(A_HEADS)], axis=0)
    out_ref[...] = out_t.T.astype(out_ref.dtype)


def _dsa(ua, q_gain, k_gain, b, s, tq):
    t = b * s
    nq = s // tq
    k_sel = min(TOPK_MAX, s // 4)
    kern = functools.partial(_dsa_kernel, tq=tq, k_sel=k_sel)
    return pl.pallas_call(
        kern,
        grid=(b, nq),
        in_specs=[
            pl.BlockSpec((tq, A_WIDTH), lambda bi, qi: (bi * nq + qi, 0)),
            pl.BlockSpec((tq, 256), lambda bi, qi: (bi * nq + qi, 2)),
            pl.BlockSpec((tq, 128), lambda bi, qi: (bi * nq + qi, 7)),
            pl.BlockSpec((s, 128), lambda bi, qi: (bi, 6)),
            pl.BlockSpec((s, 128), lambda bi, qi: (bi, 7)),
            _resident((1, A_HEAD_DIM)), _resident((1, A_KV_DIM)),
        ],
        out_specs=pl.BlockSpec((tq, A_WIDTH), lambda bi, qi: (bi * nq + qi, 0)),
        out_shape=jax.ShapeDtypeStruct((t, A_WIDTH), BF16),
        scratch_shapes=[pltpu.VMEM((nq, tq, tq), jnp.int32), pltpu.VMEM((nq, tq, tq), F32),
                        pltpu.VMEM((s, A_KV_DIM), BF16), pltpu.VMEM((s, A_KV_DIM), BF16)],
        compiler_params=_params("parallel", "arbitrary"), name="dsa",
    )(ua, ua, ua, ua, ua, q_gain, k_gain)


def _rwkv_kernel(ub_ref, mu_ref, w0_ref, a0_ref, kkw_ref, ka_ref, w2_ref, a2_ref, g2_ref, rk_ref, gng_ref,
                 gnb_ref, out_ref, state_scr, carry_scr, y_scr, *, lc, nsub):
    n = B_HEAD_DIM
    assert lc == n
    wd = B_WIDTH
    gw = MXU_TILE
    ngrp = wd // gw
    rows = lc * nsub
    c = pl.program_id(1)

    @pl.when(c == 0)
    def _():
        state_scr[...] = jnp.zeros_like(state_scr)
        carry_scr[...] = jnp.zeros_like(carry_scr)

    u = ub_ref[...]
    prev = pltpu.roll(u, 1, 0)
    prev = jnp.where(_iota((rows, 1), 0) == 0, carry_scr[...], prev)
    carry_scr[...] = u[rows - 1:rows, :]
    us = u + mu_ref[...] * (prev - u)
    r = us[:, :wd]
    k_in = us[:, wd:2 * wd]
    v = us[:, 2 * wd:3 * wd]
    o = 3 * wd
    xw = us[:, o:o + DECAY_LORA]
    xa = us[:, o + DECAY_LORA:o + DECAY_LORA + AAA_LORA]
    xg = us[:, o + DECAY_LORA + AAA_LORA:]
    w_raw = -_softplus(-(w0_ref[...] + _dot(jnp.tanh(xw), w2_ref[...]))) - 0.5
    lw = -jnp.exp(w_raw)
    a = _sigmoid(a0_ref[...] + _dot(xa, a2_ref[...]))
    g = _dot(_sigmoid(xg), g2_ref[...])
    ones_g = jnp.where(_same_group((gw, gw), n), 1.0, 0.0).astype(BF16)
    kk = k_in * kkw_ref[...]
    nrm2 = jnp.concatenate([_group_sum(kk[:, i * gw:(i + 1) * gw] ** 2, ones_g) for i in range(ngrp)], axis=1)
    kk = kk / jnp.maximum(jnp.sqrt(nrm2), 1e-12)
    k = k_in * (1.0 + (a - 1.0) * ka_ref[...])

    ri = _iota((rows, rows), 0)
    ci = _iota((rows, rows), 1)
    incl_rows = _same_group((rows, rows), lc) & (ri >= ci)
    cum = _dot_exact_lhs(jnp.where(incl_rows, 1.0, 0.0), lw)
    gam = jnp.exp(cum)
    gam_inv = jnp.exp(-cum)
    at = -kk * jnp.exp(cum - lw)
    bt = kk * a * gam_inv
    kt = k * gam_inv
    rt = r * gam

    bdmask = _same_group((gw, gw), n)
    lane_s = _iota((lc, gw), 1) & (n - 1)
    row_t = _iota((lc, gw), 0)
    strict = row_t > lane_s
    incl = row_t >= lane_s
    eye = jnp.where(row_t == lane_s, 1.0, 0.0)

    def bd(x):
        return jnp.where(bdmask, jnp.concatenate([x] * (gw // lc), axis=0), 0.0).astype(BF16)

    def stack(top, bottom):
        return jnp.concatenate([top, bottom], axis=0).astype(BF16)

    insts = [(j, i) for j in range(nsub) for i in range(ngrp)]

    def blk(x, j, i):
        return x[j * lc:(j + 1) * lc, i * gw:(i + 1) * gw]

    a_ab, a_ak, a_qb, a_qk = {}, {}, {}, {}
    for ji in insts:
        ar = stack(blk(at, *ji), blk(rt, *ji))
        pb = _dot_nt(ar, bd(blk(bt, *ji)))
        pk = _dot_nt(ar, bd(blk(kt, *ji)))
        a_ab[ji] = jnp.where(strict, pb[:lc], 0.0)
        a_ak[ji] = jnp.where(strict, pk[:lc], 0.0)
        a_qb[ji] = jnp.where(incl, pb[lc:], 0.0)
        a_qk[ji] = jnp.where(incl, pk[lc:], 0.0)

    x = {ji: eye + a_ab[ji] for ji in insts}
    pw = {ji: _dot(a_ab[ji], bd(a_ab[ji])) for ji in insts}
    for _ in range(_log2(lc) - 2):
        for ji in insts:
            res = jnp.dot(stack(x[ji], pw[ji]), bd(pw[ji]), preferred_element_type=F32)
            x[ji] = x[ji] + res[:lc]
            pw[ji] = res[lc:]
    for ji in insts:
        x[ji] = x[ji] + _dot(x[ji], bd(pw[ji]))

    w_mat, u_v, y_intra, kv_new, be, gl = {}, {}, {}, {}, {}, {}
    for ji in insts:
        j, i = ji
        t2 = jnp.dot(stack(a_ak[ji], a_qk[ji]), bd(blk(v, *ji)), preferred_element_type=F32)
        y_intra[ji] = t2[lc:]
        w_mat[ji] = _dot(x[ji], bd(blk(at, *ji)))
        u_v[ji] = _dot(x[ji], bd(t2[:lc]))
        gl[ji] = gam[(j + 1) * lc - 1:(j + 1) * lc, i * gw:(i + 1) * gw]
        be[ji] = blk(bt, *ji) * gl[ji]
        kv_new[ji] = jnp.where(bdmask, _dot_tn(blk(v, *ji), blk(kt, *ji) * gl[ji]), 0.0)

    state = [state_scr[i] for i in range(ngrp)]
    for ji in insts:
        j, i = ji
        s0 = state[i]
        r2 = _dot_nt(stack(w_mat[ji], blk(rt, *ji)), s0)
        uu = r2[:lc] + u_v[ji]
        y_scr[j * lc:(j + 1) * lc, i * gw:(i + 1) * gw] = r2[lc:] + _dot(a_qb[ji], bd(uu)) + y_intra[ji]
        state[i] = s0 * gl[ji] + jnp.where(bdmask, _dot_tn(uu, be[ji]), 0.0) + kv_new[ji]
    for i in range(ngrp):
        state_scr[i] = state[i]

    for i in range(ngrp):
        cols = slice(i * gw, (i + 1) * gw)
        y = y_scr[:, cols]
        mean = _group_sum(y, ones_g) * (1.0 / n)
        yc = y - mean
        var = _group_sum(yc * yc, ones_g) * (1.0 / n)
        yn = yc * lax.rsqrt(var + RWKV_GN_EPS) * gng_ref[:, cols] + gnb_ref[:, cols]
        bonus = _group_sum(r[:, cols] * k[:, cols] * rk_ref[:, cols], ones_g) * v[:, cols]
        out_ref[:, cols] = ((yn + bonus) * g[:, cols]).astype(out_ref.dtype)


def _rwkv(ub, mu, w0, a0, k_k, k_a, w2, a2, g2, r_k, gn_g, gn_b, b, s, lc, nsub):
    t = b * s
    rows = lc * nsub
    assert s % rows == 0
    nc = s // rows
    kern = functools.partial(_rwkv_kernel, lc=lc, nsub=nsub)
    vec = _resident((1, B_WIDTH))
    ngrp = B_WIDTH // MXU_TILE
    return pl.pallas_call(
        kern,
        grid=(b, nc),
        in_specs=[pl.BlockSpec((rows, B_TOTAL), lambda bi, ci: (bi * nc + ci, 0)), _resident((1, B_TOTAL)),
                  vec, vec, vec, vec, _resident(w2.shape), _resident(a2.shape), _resident(g2.shape),
                  vec, vec, vec],
        out_specs=pl.BlockSpec((rows, B_WIDTH), lambda bi, ci: (bi * nc + ci, 0)),
        out_shape=jax.ShapeDtypeStruct((t, B_WIDTH), BF16),
        scratch_shapes=[pltpu.VMEM((ngrp, MXU_TILE, MXU_TILE), F32), pltpu.VMEM((1, B_TOTAL), F32),
                        pltpu.VMEM((rows, B_WIDTH), F32)],
        compiler_params=_params("parallel", "arbitrary"), name="rwkv",
    )(ub, mu, w0, a0, k_k, k_a, w2, a2, g2, r_k, gn_g, gn_b)


def _outproj_kernel(x_ref, ya_ref, yb_ref, wa_ref, wb_ref, o_ref):
    o_ref[...] = (x_ref[...] + jnp.dot(ya_ref[...], wa_ref[...], preferred_element_type=F32)
                  + jnp.dot(yb_ref[...], wb_ref[...], preferred_element_type=F32))


def _outproj(x, ya, yb, wa, wb, tm):
    t = x.shape[0]
    return pl.pallas_call(
        _outproj_kernel,
        grid=(t // tm,),
        in_specs=[pl.BlockSpec((tm, D_MODEL), lambda i: (i, 0)), pl.BlockSpec((tm, A_WIDTH), lambda i: (i, 0)),
                  pl.BlockSpec((tm, B_WIDTH), lambda i: (i, 0)), _resident(wa.shape), _resident(wb.shape)],
        out_specs=pl.BlockSpec((tm, D_MODEL), lambda i: (i, 0)),
        out_shape=jax.ShapeDtypeStruct((t, D_MODEL), F32),
        compiler_params=_params("parallel"), name="outproj",
    )(x, ya, yb, wa, wb)


def _ffn_kernel(x_ref, g_ref, wg_ref, wu_ref, wd_ref, o_ref, *, th):
    x = x_ref[...]
    h = _rms(x, g_ref[...]).astype(BF16)
    acc = x
    for j in range(FFN_HIDDEN // th):
        cols = slice(j * th, (j + 1) * th)
        gate = jnp.dot(h, wg_ref[:, cols], preferred_element_type=F32)
        up = jnp.dot(h, wu_ref[:, cols], preferred_element_type=F32)
        acc = acc + jnp.dot((_silu(gate) * up).astype(BF16), wd_ref[cols, :], preferred_element_type=F32)
    o_ref[...] = acc


def _ffn(x, g, wg, wu, wd, tm, th):
    t = x.shape[0]
    return pl.pallas_call(
        functools.partial(_ffn_kernel, th=th),
        grid=(t // tm,),
        in_specs=[pl.BlockSpec((tm, D_MODEL), lambda i: (i, 0)), _resident((1, D_MODEL)),
                  _resident(wg.shape), _resident(wu.shape), _resident(wd.shape)],
        out_specs=pl.BlockSpec((tm, D_MODEL), lambda i: (i, 0)),
        out_shape=jax.ShapeDtypeStruct((t, D_MODEL), F32),
        compiler_params=_params("parallel"), name="ffn",
    )(x, g, wg, wu, wd)


def _ple_kernel(x_ref, p_ref, g_ref, wp_ref, wg_ref, o_ref):
    x = x_ref[...]
    gate = _sigmoid(jnp.dot(_rms(x, g_ref[...]).astype(BF16), wg_ref[...], preferred_element_type=F32))
    o_ref[...] = x + jnp.dot(p_ref[...].astype(BF16), wp_ref[...], preferred_element_type=F32) * gate


def _ple(x, p, g, wp, wg, tm):
    t = x.shape[0]
    return pl.pallas_call(
        _ple_kernel,
        grid=(t // tm,),
        in_specs=[pl.BlockSpec((tm, D_MODEL), lambda i: (i, 0)), pl.BlockSpec((tm, PLE_DIM), lambda i: (i, 0)),
                  _resident((1, D_MODEL)), _resident(wp.shape), _resident(wg.shape)],
        out_specs=pl.BlockSpec((tm, D_MODEL), lambda i: (i, 0)),
        out_shape=jax.ShapeDtypeStruct((t, D_MODEL), F32),
        compiler_params=_params("parallel"), name="ple",
    )(x, p, g, wp, wg)


def _upproj_kernel(x_ref, g_ref, w_ref, xm_ref, z_ref):
    h = _rms(x_ref[...], g_ref[...]).astype(BF16)
    xm_ref[...] = jnp.dot(h, w_ref[:, :C_INNER], preferred_element_type=F32).astype(xm_ref.dtype)
    z_ref[...] = jnp.dot(h, w_ref[:, C_INNER:], preferred_element_type=F32).astype(z_ref.dtype)


def _upproj(x, g, w, tm):
    t = x.shape[0]
    return pl.pallas_call(
        _upproj_kernel,
        grid=(t // tm,),
        in_specs=[pl.BlockSpec((tm, D_MODEL), lambda i: (i, 0)), _resident((1, D_MODEL)), _resident(w.shape)],
        out_specs=[pl.BlockSpec((tm, C_INNER), lambda i: (i, 0))] * 2,
        out_shape=[jax.ShapeDtypeStruct((t, C_INNER), BF16)] * 2,
        compiler_params=_params("parallel"), name="upproj",
    )(x, g, w)


def _conv_qkv_kernel(xm_ref, cw_ref, cb_ref, wq_ref, wk_ref, wv_ref, wif_ref, bif_ref,
                     xc_o, q_o, k_o, v_o, gcol_o, grow_o, carry_scr, *, tm, tiles_per_seq):
    i = pl.program_id(0)

    @pl.when(i % tiles_per_seq == 0)
    def _():
        carry_scr[...] = jnp.zeros_like(carry_scr)

    xmb = xm_ref[...]
    xm = xmb.astype(F32)
    carry = carry_scr[...]
    row8 = _iota((8, 1), 0)
    acc = cb_ref[...] + cw_ref[C_CONV - 1:C_CONV, :] * xm
    for sft in range(1, C_CONV):
        rolled = pltpu.roll(xm, sft, 0)
        head = jnp.where(row8 < sft, pltpu.roll(carry, sft, 0), rolled[:8])
        shifted = jnp.concatenate([head, rolled[8:]], axis=0)
        acc = acc + cw_ref[C_CONV - 1 - sft:C_CONV - sft, :] * shifted
    carry_scr[...] = xm[tm - 8:, :]
    xcb = _silu(acc).astype(BF16)
    xc_o[...] = xcb

    gates = jnp.zeros((tm, GATE_PAD), F32)
    for j in range(C_INNER // QKV_TILE):
        cols = slice(j * QKV_TILE, (j + 1) * QKV_TILE)
        qj = jnp.dot(xcb[:, cols], wq_ref[j], preferred_element_type=F32).astype(BF16)
        kj = jnp.dot(xcb[:, cols], wk_ref[j], preferred_element_type=F32).astype(BF16)
        vj = jnp.dot(xmb[:, cols], wv_ref[j], preferred_element_type=F32).astype(BF16)
        q_o[:, cols] = qj
        k_o[:, cols] = kj
        v_o[:, cols] = vj
        gates = gates + (jnp.dot(qj, wif_ref[0, cols, :], preferred_element_type=F32)
                         + jnp.dot(kj, wif_ref[1, cols, :], preferred_element_type=F32)
                         + jnp.dot(vj, wif_ref[2, cols, :], preferred_element_type=F32))
    gates = gates + bif_ref[...]
    lane = _iota((tm, GATE_PAD), 1)
    gates = jnp.where(lane < C_HEADS, gates, -_softplus(-gates))
    gcol_o[...] = gates[:, :2 * C_HEADS]
    grow_o[...] = gates.T[:2 * C_HEADS, :]


def _conv_qkv(xm, cw, cb, wq, wk, wv, wif, bif, s, tm):
    t = xm.shape[0]
    kern = functools.partial(_conv_qkv_kernel, tm=tm, tiles_per_seq=s // tm)
    row = pl.BlockSpec((tm, C_INNER), lambda i: (i, 0))
    return pl.pallas_call(
        kern,
        grid=(t // tm,),
        in_specs=[row, _resident(cw.shape), _resident(cb.shape), _resident(wq.shape), _resident(wk.shape),
                  _resident(wv.shape), _resident(wif.shape), _resident(bif.shape)],
        out_specs=[row, row, row, row, pl.BlockSpec((tm, 2 * C_HEADS), lambda i: (i, 0)),
                   pl.BlockSpec((2 * C_HEADS, tm), lambda i: (0, i))],
        out_shape=[jax.ShapeDtypeStruct((t, C_INNER), BF16)] * 4
        + [jax.ShapeDtypeStruct((t, 2 * C_HEADS), F32), jax.ShapeDtypeStruct((2 * C_HEADS, t), F32)],
        scratch_shapes=[pltpu.VMEM((8, C_INNER), F32)],
        compiler_params=_params("arbitrary"), name="conv_qkv",
    )(xm, cw, cb, wq, wk, wv, wif, bif)


def _mlstm_kernel(q_ref, k_ref, v_ref, gcol_ref, grow_ref, h_ref, c_scr, n_scr, m_scr, *, lc):
    c = pl.program_id(1)

    @pl.when(c == 0)
    def _():
        c_scr[...] = jnp.zeros_like(c_scr)
        n_scr[...] = jnp.zeros_like(n_scr)
        m_scr[...] = jnp.zeros_like(m_scr)

    d_h = C_HEAD_DIM
    rows = _iota((lc, lc), 0)
    cols = _iota((lc, lc), 1)
    tril = rows >= cols
    gcol = gcol_ref[...]
    grow = grow_ref[...]
    bcol = _dot_exact_lhs(jnp.where(tril, 1.0, 0.0), gcol)
    brow = _dot_exact_rhs(grow, jnp.where(rows <= cols, 1.0, 0.0))

    for h in range(C_HEADS):
        hs = slice(h * d_h, (h + 1) * d_h)
        b_c = bcol[:, C_HEADS + h:C_HEADS + h + 1]
        b_r = brow[C_HEADS + h:C_HEADS + h + 1, :]
        i_c = gcol[:, h:h + 1]
        i_r = grow[h:h + 1, :]
        m_prev = m_scr[h][0:1, 0:1]

        dmat = jnp.where(tril, b_c - b_r + i_r, -jnp.inf)
        inter = b_c + m_prev
        m_t = jnp.maximum(inter, jnp.max(dmat, axis=-1, keepdims=True))
        wts = jnp.exp(dmat - m_t)
        sc = jnp.exp(inter - m_t)

        qb = q_ref[:, hs]
        qh = qb.astype(F32)
        kh = k_ref[:, hs].astype(F32) * (d_h ** -0.5)
        vh = v_ref[:, hs]
        qk = _dot_nt(qb, kh) * wts
        c_prev = c_scr[h]
        n_prev = n_scr[h][0:1, :]
        num = jnp.dot(qk.astype(BF16), vh, preferred_element_type=F32) + sc * _dot(qb, c_prev)
        den = jnp.sum(qk, axis=-1, keepdims=True) + sc * jnp.sum(qh * n_prev, axis=-1, keepdims=True)
        h_ref[:, hs] = (num / jnp.maximum(jnp.abs(den), jnp.exp(-m_t))).astype(h_ref.dtype)

        b_last = b_r[:, lc - 1:lc]
        m_new = jnp.maximum(b_last + m_prev, jnp.max(b_last - b_r + i_r, axis=-1, keepdims=True))
        dc = jnp.exp(b_last + m_prev - m_new)
        kws = kh * jnp.exp(b_last - b_c + i_c - m_new)
        c_scr[h] = dc * c_prev + _dot_tn(kws, vh)
        n_scr[h] = jnp.broadcast_to(dc * n_prev + jnp.sum(kws, axis=0, keepdims=True), (8, d_h))
        m_scr[h] = jnp.broadcast_to(m_new, (8, 128))


def _mlstm(q, k, v, gcol, grow, b, s, lc):
    t = b * s
    nc = s // lc
    row = pl.BlockSpec((lc, C_INNER), lambda bi, ci: (bi * nc + ci, 0))
    return pl.pallas_call(
        functools.partial(_mlstm_kernel, lc=lc),
        grid=(b, nc),
        in_specs=[row, row, row, pl.BlockSpec((lc, 2 * C_HEADS), lambda bi, ci: (bi * nc + ci, 0)),
                  pl.BlockSpec((2 * C_HEADS, lc), lambda bi, ci: (0, bi * nc + ci))],
        out_specs=row,
        out_shape=jax.ShapeDtypeStruct((t, C_INNER), BF16),
        scratch_shapes=[pltpu.VMEM((C_HEADS, C_HEAD_DIM, C_HEAD_DIM), F32),
                        pltpu.VMEM((C_HEADS, 8, C_HEAD_DIM), F32), pltpu.VMEM((C_HEADS, 8, 128), F32)],
        compiler_params=_params("parallel", "arbitrary"), name="mlstm",
    )(q, k, v, gcol, grow)


def _downproj_kernel(x_ref, h_ref, xc_ref, z_ref, mg_ref, sk_ref, w_ref, o_ref):
    hcat = []
    for h in range(C_HEADS):
        y = h_ref[:, h * C_HEAD_DIM:(h + 1) * C_HEAD_DIM].astype(F32)
        mean = jnp.mean(y, axis=-1, keepdims=True)
        yc = y - mean
        var = jnp.mean(yc * yc, axis=-1, keepdims=True)
        hcat.append(yc * lax.rsqrt(var + C_GN_EPS))
    hn = jnp.concatenate(hcat, axis=-1) * mg_ref[...]
    y = (hn + sk_ref[...] * xc_ref[...].astype(F32)) * _silu(z_ref[...].astype(F32))
    o_ref[...] = x_ref[...] + jnp.dot(y.astype(BF16), w_ref[...], preferred_element_type=F32)


def _downproj(x, h, xc, z, mh_g, skip, w, tm):
    t = x.shape[0]
    row = pl.BlockSpec((tm, C_INNER), lambda i: (i, 0))
    return pl.pallas_call(
        _downproj_kernel,
        grid=(t // tm,),
        in_specs=[pl.BlockSpec((tm, D_MODEL), lambda i: (i, 0)), row, row, row,
                  _resident((1, C_INNER)), _resident((1, C_INNER)), _resident(w.shape)],
        out_specs=pl.BlockSpec((tm, D_MODEL), lambda i: (i, 0)),
        out_shape=jax.ShapeDtypeStruct((t, D_MODEL), F32),
        compiler_params=_params("parallel"), name="downproj",
    )(x, h, xc, z, mh_g, skip, w)


def _pack_w_in(w_in):
    q, k, v, iq, ik, iw = jnp.split(w_in[:, :A_TOTAL], [512, 576, 640, 896, 960], axis=1)
    pad = jnp.zeros((D_MODEL, A_PAD - A_TOTAL), w_in.dtype)
    wa = jnp.concatenate([q, iq, k, v, ik, iw, pad], axis=1)
    return wa.astype(BF16), w_in[:, A_TOTAL:].astype(BF16)


def _block_diag_tiles(w):
    per = QKV_TILE // C_QKV_BLOCK
    wt = w.reshape(C_INNER // QKV_TILE, per, C_QKV_BLOCK, C_QKV_BLOCK)
    eye = jnp.eye(per, dtype=w.dtype)
    full = wt[:, :, :, None, :] * eye[None, :, None, :, None]
    return full.reshape(C_INNER // QKV_TILE, QKV_TILE, QKV_TILE).astype(BF16)


def _row_tile(s, want):
    return min(want, s)


def kernel(x, p, mix_norm, a_q_gain, a_k_gain, w_in_e, b_mu, b_w0, b_w2, b_a0, b_a2, b_g2, b_k_k, b_k_a,
           b_r_k, b_gn_g, b_gn_b, w_out_e, c_w_up, c_conv_w, c_conv_b, c_wq, c_wk, c_wv, c_w_if, c_b_i,
           c_b_f, c_mh_g, c_skip, c_w_down, ffn_norm, ffn_w_gate, ffn_w_up, ffn_w_down, ple_w, ple_norm,
           ple_w_gate):
    b, s, d = x.shape
    t = b * s
    depth = p.shape[0]
    tm = _row_tile(s, 512)
    xf = x.reshape(t, d)
    pf = p.reshape(depth, t, PLE_DIM)
    vec = lambda a: a.reshape(1, -1)

    for i in range(depth):
        j = i // 2
        if i % 2 == 0:
            wa, wb = _pack_w_in(w_in_e[j])
            ua, ub = _inproj(xf, vec(mix_norm[i]), wa, wb, tm)
            ya = _dsa(ua, vec(a_q_gain[j]), vec(a_k_gain[j]), b, s, _row_tile(s, 256))
            yb = _rwkv(ub, vec(b_mu[j]), vec(b_w0[j]), vec(b_a0[j]), vec(b_k_k[j]), vec(b_k_a[j]),
                       b_w2[j].astype(BF16), b_a2[j].astype(BF16), b_g2[j].astype(BF16), vec(b_r_k[j]),
                       vec(b_gn_g[j]), vec(b_gn_b[j]), b, s, CHUNK, 4)
            wo = w_out_e[j].astype(BF16)
            xf = _outproj(xf, ya, yb, wo[:A_WIDTH], wo[A_WIDTH:], tm)
        else:
            xm, z = _upproj(xf, vec(mix_norm[i]), c_w_up[j].astype(BF16), tm)
            wif = jnp.pad(c_w_if[j], ((0, 0), (0, GATE_PAD - 2 * C_HEADS))).reshape(3, C_INNER, GATE_PAD)
            bif = jnp.pad(jnp.concatenate([c_b_i[j], c_b_f[j]]), (0, GATE_PAD - 2 * C_HEADS)).reshape(1, GATE_PAD)
            xc, q, k, v, gcol, grow = _conv_qkv(
                xm, c_conv_w[j].reshape(C_CONV, C_INNER), vec(c_conv_b[j]), _block_diag_tiles(c_wq[j]),
                _block_diag_tiles(c_wk[j]), _block_diag_tiles(c_wv[j]), wif.astype(BF16), bif, s,
                _row_tile(s, 256))
            hcell = _mlstm(q, k, v, gcol, grow, b, s, _row_tile(s, 256))
            xf = _downproj(xf, hcell, xc, z, vec(c_mh_g[j]), vec(c_skip[j]), c_w_down[j].astype(BF16), tm)
        xf = _ffn(xf, vec(ffn_norm[i]), ffn_w_gate[i].astype(BF16), ffn_w_up[i].astype(BF16),
                  ffn_w_down[i].astype(BF16), tm, 256)
        xf = _ple(xf, pf[i], vec(ple_norm[i]), ple_w[i].astype(BF16), ple_w_gate[i].astype(BF16), tm)
    return xf.reshape(b, s, d)
```

```python
import functools
import math

import jax
import jax.numpy as jnp
from jax import lax
from jax.experimental import pallas as pl
from jax.experimental.pallas import tpu as pltpu

F32 = jnp.float32
BF16 = jnp.bfloat16

D_MODEL = 1024
CHUNK = 64
PLE_DIM = 256
NORM_EPS = 1e-6
A_HEADS = 8
A_HEAD_DIM = 64
A_KV_DIM = 64
A_WIDTH = A_HEADS * A_HEAD_DIM
IDX_HEADS = 4
IDX_DIM = 64
TOPK_MAX = 256
B_HEADS = 8
B_HEAD_DIM = 64
B_WIDTH = B_HEADS * B_HEAD_DIM
DECAY_LORA = 64
AAA_LORA = 64
GATE_LORA = 128
RWKV_GN_EPS = 64e-5
A_TOTAL = A_WIDTH + 2 * A_KV_DIM + IDX_HEADS * IDX_DIM + IDX_DIM + IDX_HEADS
B_TOTAL = 3 * B_WIDTH + DECAY_LORA + AAA_LORA + GATE_LORA
A_PAD = 1024
C_INNER = 2 * D_MODEL
C_HEADS = 4
C_HEAD_DIM = C_INNER // C_HEADS
C_CONV = 4
C_QKV_BLOCK = 4
C_GN_EPS = 1e-5
FFN_HIDDEN = 2816
MXU_TILE = 256
QKV_TILE = MXU_TILE
GATE_PAD = 128
RWKV_GROUP_HEADS = MXU_TILE // B_HEAD_DIM

VMEM_LIMIT = 56 * 1024 * 1024


def _params(*sem):
    return pltpu.CompilerParams(dimension_semantics=sem, vmem_limit_bytes=VMEM_LIMIT)


def _resident(shape):
    nd = len(shape)
    return pl.BlockSpec(shape, lambda *_: (0,) * nd, pipeline_mode=pl.Buffered(1))


def _rms(x, g):
    return x * lax.rsqrt(jnp.mean(x * x, axis=-1, keepdims=True) + NORM_EPS) * g


def _dot(a, b):
    return jnp.dot(a.astype(BF16), b.astype(BF16), preferred_element_type=F32)


def _dot_nt(a, b):
    return lax.dot_general(a.astype(BF16), b.astype(BF16), (((1,), (1,)), ((), ())),
                           preferred_element_type=F32)


def _dot_tn(a, b):
    return lax.dot_general(a.astype(BF16), b.astype(BF16), (((0,), (0,)), ((), ())),
                           preferred_element_type=F32)


def _split3(x):
    hi = x.astype(BF16)
    r1 = x - hi.astype(F32)
    mid = r1.astype(BF16)
    lo = (r1 - mid.astype(F32)).astype(BF16)
    return hi, mid, lo


def _dot_exact_lhs(a_exact, x):
    a = a_exact.astype(BF16)
    hi, mid, lo = _split3(x)
    return (jnp.dot(a, hi, preferred_element_type=F32) + jnp.dot(a, mid, preferred_element_type=F32)
            + jnp.dot(a, lo, preferred_element_type=F32))


def _dot_exact_rhs(x, b_exact):
    b = b_exact.astype(BF16)
    hi, mid, lo = _split3(x)
    return (jnp.dot(hi, b, preferred_element_type=F32) + jnp.dot(mid, b, preferred_element_type=F32)
            + jnp.dot(lo, b, preferred_element_type=F32))


def _group_sum(x, ones):
    hi = x.astype(BF16)
    lo = (x - hi.astype(F32)).astype(BF16)
    return jnp.dot(hi, ones, preferred_element_type=F32) + jnp.dot(lo, ones, preferred_element_type=F32)


def _sigmoid(x):
    return 1.0 / (1.0 + jnp.exp(-x))


def _softplus(x):
    return jnp.maximum(x, 0.0) + jnp.log(1.0 + jnp.exp(-jnp.abs(x)))


def _silu(x):
    return x * _sigmoid(x)


def _iota(shape, dim):
    return lax.broadcasted_iota(jnp.int32, shape, dim)


def _log2(n):
    assert n & (n - 1) == 0
    return n.bit_length() - 1


def _same_group(shape, group):
    sh = _log2(group)
    return jnp.right_shift(_iota(shape, 0), sh) == jnp.right_shift(_iota(shape, 1), sh)


def _inproj_kernel(x_ref, g_ref, wa_ref, wb_ref, ua_ref, ub_ref):
    h = _rms(x_ref[...], g_ref[...]).astype(BF16)
    ua_ref[...] = jnp.dot(h, wa_ref[...], preferred_element_type=F32)
    ub_ref[...] = jnp.dot(h, wb_ref[...], preferred_element_type=F32)


def _inproj(x, g, wa, wb, tm):
    t = x.shape[0]
    return pl.pallas_call(
        _inproj_kernel,
        grid=(t // tm,),
        in_specs=[pl.BlockSpec((tm, D_MODEL), lambda i: (i, 0)), _resident((1, D_MODEL)),
                  _resident(wa.shape), _resident(wb.shape)],
        out_specs=[pl.BlockSpec((tm, A_PAD), lambda i: (i, 0)),
                   pl.BlockSpec((tm, B_TOTAL), lambda i: (i, 0))],
        out_shape=[jax.ShapeDtypeStruct((t, A_PAD), F32), jax.ShapeDtypeStruct((t, B_TOTAL), F32)],
        compiler_params=_params("parallel"), name="inproj",
    )(x, g, wa, wb)


def _dsa_kernel(q_ref, iq_ref, iwq_ref, kv_ref, ikw_ref, qg_ref, kg_ref, out_ref, key_scr, sel_scr, kn_scr,
                v_scr, *, tq, k_sel):
    tk = tq
    qi = pl.program_id(1)
    nkt = qi + 1
    kf = jnp.float32(k_sel)
    needs_selection = (qi + 1) * tq > k_sel

    @pl.when(qi == 0)
    def _():
        kv = kv_ref[...]
        kraw = kv[:, :A_KV_DIM]
        kn_scr[...] = (kraw * lax.rsqrt(jnp.mean(kraw * kraw, axis=-1, keepdims=True) + NORM_EPS)
                       * kg_ref[...]).astype(BF16)
        v_scr[...] = kv[:, A_KV_DIM:2 * A_KV_DIM].astype(BF16)

    qpos = qi * tq + _iota((1, tq), 1)
    limit = (jnp.right_shift(qpos, _log2(CHUNK)) + 1) * CHUNK

    def valid_tile(kt):
        return kt * tk + _iota((tk, tq), 0) < limit

    def tile_rows(ref, kt):
        return ref[pl.ds(pl.multiple_of(kt * tk, tk), tk), :]

    @pl.when(needs_selection)
    def _():
        iq = iq_ref[...].astype(BF16)
        iq_h = [iq[:, h * IDX_DIM:(h + 1) * IDX_DIM] for h in range(IDX_HEADS)]
        pick = jnp.where((_iota((8, 128), 1) == _iota((8, 128), 0) + IDX_DIM) & (_iota((8, 128), 0) < IDX_HEADS),
                         1.0, 0.0).astype(BF16)
        iw_t = sum(lax.dot_general(pick, part, (((1,), (1,)), ((), ())), preferred_element_type=F32)
                   for part in _split3(iwq_ref[...])) * (IDX_HEADS ** -0.5 * IDX_DIM ** -0.5)

        def score_body(kt, _):
            ik_t = tile_rows(ikw_ref, kt)[:, :IDX_DIM].astype(BF16)
            sc = jnp.zeros((tk, tq), F32)
            for h in range(IDX_HEADS):
                sc = sc + iw_t[h:h + 1, :] * jnp.maximum(_dot_nt(ik_t, iq_h[h]), 0.0)
            sc = jnp.where(valid_tile(kt), sc + 0.0, -jnp.inf)
            bits = pltpu.bitcast(sc, jnp.int32)
            key_scr[kt] = jnp.where(bits < 0, bits ^ jnp.int32(0x7FFFFFFF), bits)
            return 0

        lax.fori_loop(0, nkt, score_body, 0)

        def count_ge(cand):
            def body(kt, acc):
                hit = jnp.where(key_scr[kt] >= cand, jnp.float32(1.0), jnp.float32(0.0))
                return acc + jnp.sum(hit.reshape(tk // 8, 8, tq), axis=0)
            return jnp.sum(lax.fori_loop(0, nkt, body, jnp.zeros((8, tq), F32)), axis=0, keepdims=True)

        ans = jnp.where(count_ge(jnp.zeros((1, tq), jnp.int32)) >= kf, jnp.int32(0), jnp.int32(-2 ** 31))

        def bit_step(i, ans):
            cand = ans | jnp.left_shift(jnp.int32(1), jnp.int32(30) - i)
            return jnp.where(count_ge(cand) >= kf, cand, ans)

        thr = lax.fori_loop(0, 31, bit_step, ans)
        need = kf - count_ge(thr + 1)
        tril = jnp.where(_iota((tk, tk), 0) >= _iota((tk, tk), 1), 1.0, 0.0).astype(BF16)

        def tie_body(kt, carry):
            key = key_scr[kt]
            eq = key == thr
            rank = carry + jnp.dot(tril, jnp.where(eq, 1.0, 0.0).astype(BF16), preferred_element_type=F32)
            sel = ((key > thr) | (eq & (rank <= need))) & valid_tile(kt)
            sel_scr[kt] = jnp.where(sel, 0.0, -jnp.inf)
            return rank[tk - 1:tk, :]

        lax.fori_loop(0, nkt, tie_body, jnp.zeros((1, tq), F32))

    @pl.when(jnp.logical_not(needs_selection))
    def _():
        def body(kt, _):
            sel_scr[kt] = jnp.where(valid_tile(kt), 0.0, -jnp.inf)
            return 0
        lax.fori_loop(0, nkt, body, 0)

    q = q_ref[...]
    qscale = A_HEAD_DIM ** -0.5 * math.log2(math.e)
    qn = []
    for h in range(A_HEADS):
        qh = q[:, h * A_HEAD_DIM:(h + 1) * A_HEAD_DIM]
        qn.append((qh * lax.rsqrt(jnp.mean(qh * qh, axis=-1, keepdims=True) + NORM_EPS)
                   * (qg_ref[...] * qscale)).astype(BF16))

    def att_body(kt, carry):
        ms, ls, accs = carry
        kn_t = tile_rows(kn_scr, kt)
        v_t = tile_rows(v_scr, kt)
        neg = sel_scr[kt]
        sc = [_dot_nt(kn_t, qn[h]) + neg for h in range(A_HEADS)]
        m_new = [jnp.maximum(ms[h], jnp.max(sc[h], axis=0, keepdims=True)) for h in range(A_HEADS)]
        p = [jnp.exp2(sc[h] - m_new[h]) for h in range(A_HEADS)]
        alpha = [jnp.exp2(ms[h] - m_new[h]) for h in range(A_HEADS)]
        ls = [alpha[h] * ls[h] + jnp.sum(p[h], axis=0, keepdims=True) for h in range(A_HEADS)]
        accs = [alpha[h] * accs[h] + _dot_tn(v_t, p[h]) for h in range(A_HEADS)]
        return m_new, ls, accs

    init = ([jnp.full((1, tq), -1e30, F32)] * A_HEADS, [jnp.zeros((1, tq), F32)] * A_HEADS,
            [jnp.zeros((A_KV_DIM, tq), F32)] * A_HEADS)
    _, ls, accs = lax.fori_loop(0, nkt, att_body, init)
    out_t = jnp.concatenate([accs[h] / ls[h] for h in range(A_HEADS)], axis=0)
    out_ref[...] = out_t.T.astype(out_ref.dtype)


def _dsa(ua, q_gain, k_gain, b, s, tq):
    t = b * s
    nq = s // tq
    k_sel = min(TOPK_MAX, s // 4)
    kern = functools.partial(_dsa_kernel, tq=tq, k_sel=k_sel)
    return pl.pallas_call(
        kern,
        grid=(b, nq),
        in_specs=[
            pl.BlockSpec((tq, A_WIDTH), lambda bi, qi: (bi * nq + qi, 0)),
            pl.BlockSpec((tq, 256), lambda bi, qi: (bi * nq + qi, 2)),
            pl.BlockSpec((tq, 128), lambda bi, qi: (bi * nq + qi, 7)),
            pl.BlockSpec((s, 128), lambda bi, qi: (bi, 6)),
            pl.BlockSpec((s, 128), lambda bi, qi: (bi, 7)),
            _resident((1, A_HEAD_DIM)), _resident((1, A_KV_DIM)),
        ],
        out_specs=pl.BlockSpec((tq, A_WIDTH), lambda bi, qi: (bi * nq + qi, 0)),
        out_shape=jax.ShapeDtypeStruct((t, A_WIDTH), BF16),
        scratch_shapes=[pltpu.VMEM((nq, tq, tq), jnp.int32), pltpu.VMEM((nq, tq, tq), F32),
                        pltpu.VMEM((s, A_KV_DIM), BF16), pltpu.VMEM((s, A_KV_DIM), BF16)],
        compiler_params=_params("parallel", "arbitrary"), name="dsa",
    )(ua, ua, ua, ua, ua, q_gain, k_gain)


def _rwkv_kernel(ub_ref, mu_ref, w0_ref, a0_ref, kkw_ref, ka_ref, w2_ref, a2_ref, g2_ref, rk_ref, gng_ref,
                 gnb_ref, out_ref, state_scr, carry_scr, y_scr, *, lc, nsub):
    n = B_HEAD_DIM
    assert lc == n
    wd = B_WIDTH
    gw = MXU_TILE
    ngrp = wd // gw
    rows = lc * nsub
    c = pl.program_id(1)

    @pl.when(c == 0)
    def _():
        state_scr[...] = jnp.zeros_like(state_scr)
        carry_scr[...] = jnp.zeros_like(carry_scr)

    u = ub_ref[...]
    prev = pltpu.roll(u, 1, 0)
    prev = jnp.where(_iota((rows, 1), 0) == 0, carry_scr[...], prev)
    carry_scr[...] = u[rows - 1:rows, :]
    us = u + mu_ref[...] * (prev - u)
    r = us[:, :wd]
    k_in = us[:, wd:2 * wd]
    v = us[:, 2 * wd:3 * wd]
    o = 3 * wd
    xw = us[:, o:o + DECAY_LORA]
    xa = us[:, o + DECAY_LORA:o + DECAY_LORA + AAA_LORA]
    xg = us[:, o + DECAY_LORA + AAA_LORA:]
    w_raw = -_softplus(-(w0_ref[...] + _dot(jnp.tanh(xw), w2_ref[...]))) - 0.5
    lw = -jnp.exp(w_raw)
    a = _sigmoid(a0_ref[...] + _dot(xa, a2_ref[...]))
    g = _dot(_sigmoid(xg), g2_ref[...])
    ones_g = jnp.where(_same_group((gw, gw), n), 1.0, 0.0).astype(BF16)
    kk = k_in * kkw_ref[...]
    nrm2 = jnp.concatenate([_group_sum(kk[:, i * gw:(i + 1) * gw] ** 2, ones_g) for i in range(ngrp)], axis=1)
    kk = kk / jnp.maximum(jnp.sqrt(nrm2), 1e-12)
    k = k_in * (1.0 + (a - 1.0) * ka_ref[...])

    ri = _iota((rows, rows), 0)
    ci = _iota((rows, rows), 1)
    incl_rows = _same_group((rows, rows), lc) & (ri >= ci)
    cum = _dot_exact_lhs(jnp.where(incl_rows, 1.0, 0.0), lw)
    gam = jnp.exp(cum)
    gam_inv = jnp.exp(-cum)
    at = -kk * jnp.exp(cum - lw)
    bt = kk * a * gam_inv
    kt = k * gam_inv
    rt = r * gam

    bdmask = _same_group((gw, gw), n)
    lane_s = _iota((lc, gw), 1) & (n - 1)
    row_t = _iota((lc, gw), 0)
    strict = row_t > lane_s
    incl = row_t >= lane_s
    eye = jnp.where(row_t == lane_s, 1.0, 0.0)

    def bd(x):
        return jnp.where(bdmask, jnp.concatenate([x] * (gw // lc), axis=0), 0.0).astype(BF16)

    def stack(top, bottom):
        return jnp.concatenate([top, bottom], axis=0).astype(BF16)

    insts = [(j, i) for j in range(nsub) for i in range(ngrp)]

    def blk(x, j, i):
        return x[j * lc:(j + 1) * lc, i * gw:(i + 1) * gw]

    a_ab, a_ak, a_qb, a_qk = {}, {}, {}, {}
    for ji in insts:
        ar = stack(blk(at, *ji), blk(rt, *ji))
        pb = _dot_nt(ar, bd(blk(bt, *ji)))
        pk = _dot_nt(ar, bd(blk(kt, *ji)))
        a_ab[ji] = jnp.where(strict, pb[:lc], 0.0)
        a_ak[ji] = jnp.where(strict, pk[:lc], 0.0)
        a_qb[ji] = jnp.where(incl, pb[lc:], 0.0)
        a_qk[ji] = jnp.where(incl, pk[lc:], 0.0)

    x = {ji: eye + a_ab[ji] for ji in insts}
    pw = {ji: _dot(a_ab[ji], bd(a_ab[ji])) for ji in insts}
    for _ in range(_log2(lc) - 2):
        for ji in insts:
            res = jnp.dot(stack(x[ji], pw[ji]), bd(pw[ji]), preferred_element_type=F32)
            x[ji] = x[ji] + res[:lc]
            pw[ji] = res[lc:]
    for ji in insts:
        x[ji] = x[ji] + _dot(x[ji], bd(pw[ji]))

    w_mat, u_v, y_intra, kv_new, be, gl = {}, {}, {}, {}, {}, {}
    for ji in insts:
        j, i = ji
        t2 = jnp.dot(stack(a_ak[ji], a_qk[ji]), bd(blk(v, *ji)), preferred_element_type=F32)
        y_intra[ji] = t2[lc:]
        w_mat[ji] = _dot(x[ji], bd(blk(at, *ji)))
        u_v[ji] = _dot(x[ji], bd(t2[:lc]))
        gl[ji] = gam[(j + 1) * lc - 1:(j + 1) * lc, i * gw:(i + 1) * gw]
        be[ji] = blk(bt, *ji) * gl[ji]
        kv_new[ji] = jnp.where(bdmask, _dot_tn(blk(v, *ji), blk(kt, *ji) * gl[ji]), 0.0)

    state = [state_scr[i] for i in range(ngrp)]
    for ji in insts:
        j, i = ji
        s0 = state[i]
        r2 = _dot_nt(stack(w_mat[ji], blk(rt, *ji)), s0)
        uu = r2[:lc] + u_v[ji]
        y_scr[j * lc:(j + 1) * lc, i * gw:(i + 1) * gw] = r2[lc:] + _dot(a_qb[ji], bd(uu)) + y_intra[ji]
        state[i] = s0 * gl[ji] + jnp.where(bdmask, _dot_tn(uu, be[ji]), 0.0) + kv_new[ji]
    for i in range(ngrp):
        state_scr[i] = state[i]

    for i in range(ngrp):
        cols = slice(i * gw, (i + 1) * gw)
        y = y_scr[:, cols]
        mean = _group_sum(y, ones_g) * (1.0 / n)
        yc = y - mean
        var = _group_sum(yc * yc, ones_g) * (1.0 / n)
        yn = yc * lax.rsqrt(var + RWKV_GN_EPS) * gng_ref[:, cols] + gnb_ref[:, cols]
        bonus = _group_sum(r[:, cols] * k[:, cols] * rk_ref[:, cols], ones_g) * v[:, cols]
        out_ref[:, cols] = ((yn + bonus) * g[:, cols]).astype(out_ref.dtype)


def _rwkv(ub, mu, w0, a0, k_k, k_a, w2, a2, g2, r_k, gn_g, gn_b, b, s, lc, nsub):
    t = b * s
    rows = lc * nsub
    assert s % rows == 0
    nc = s // rows
    kern = functools.partial(_rwkv_kernel, lc=lc, nsub=nsub)
    vec = _resident((1, B_WIDTH))
    ngrp = B_WIDTH // MXU_TILE
    return pl.pallas_call(
        kern,
        grid=(b, nc),
        in_specs=[pl.BlockSpec((rows, B_TOTAL), lambda bi, ci: (bi * nc + ci, 0)), _resident((1, B_TOTAL)),
                  vec, vec, vec, vec, _resident(w2.shape), _resident(a2.shape), _resident(g2.shape),
                  vec, vec, vec],
        out_specs=pl.BlockSpec((rows, B_WIDTH), lambda bi, ci: (bi * nc + ci, 0)),
        out_shape=jax.ShapeDtypeStruct((t, B_WIDTH), BF16),
        scratch_shapes=[pltpu.VMEM((ngrp, MXU_TILE, MXU_TILE), F32), pltpu.VMEM((1, B_TOTAL), F32),
                        pltpu.VMEM((rows, B_WIDTH), F32)],
        compiler_params=_params("parallel", "arbitrary"), name="rwkv",
    )(ub, mu, w0, a0, k_k, k_a, w2, a2, g2, r_k, gn_g, gn_b)


def _tail_kernel(*refs, n_proj, th):
    x_ref = refs[0]
    acts = refs[1:1 + n_proj]
    p_ref = refs[1 + n_proj]
    projs = refs[2 + n_proj:2 + 2 * n_proj]
    fg_ref, wg_ref, wu_ref, wd_ref, pg_ref, wp_ref, wpg_ref, o_ref = refs[2 + 2 * n_proj:]

    x = x_ref[...]
    for a_ref, w_ref in zip(acts, projs):
        x = x + jnp.dot(a_ref[...], w_ref[...], preferred_element_type=F32)
    o_ref[...] = x
    x = o_ref[...]

    h = _rms(x, fg_ref[...]).astype(BF16)
    for j in range(FFN_HIDDEN // th):
        cols = slice(j * th, (j + 1) * th)
        gate = jnp.dot(h, wg_ref[:, cols], preferred_element_type=F32)
        up = jnp.dot(h, wu_ref[:, cols], preferred_element_type=F32)
        x = x + jnp.dot((_silu(gate) * up).astype(BF16), wd_ref[cols, :], preferred_element_type=F32)

    gate = _sigmoid(jnp.dot(_rms(x, pg_ref[...]).astype(BF16), wpg_ref[...], preferred_element_type=F32))
    o_ref[...] = x + jnp.dot(p_ref[...].astype(BF16), wp_ref[...], preferred_element_type=F32) * gate


def _tail(x, acts, projs, p, fg, wg, wu, wd, pg, wp, wpg, tm, th):
    t = x.shape[0]
    rows = lambda width: pl.BlockSpec((tm, width), lambda i: (i, 0))
    vec = _resident((1, D_MODEL))
    return pl.pallas_call(
        functools.partial(_tail_kernel, n_proj=len(acts), th=th),
        grid=(t // tm,),
        in_specs=[rows(D_MODEL)] + [rows(a.shape[1]) for a in acts] + [rows(PLE_DIM)]
        + [_resident(w.shape) for w in projs]
        + [vec, _resident(wg.shape), _resident(wu.shape), _resident(wd.shape), vec, _resident(wp.shape),
           _resident(wpg.shape)],
        out_specs=rows(D_MODEL),
        out_shape=jax.ShapeDtypeStruct((t, D_MODEL), F32),
        compiler_params=_params("parallel"), name="tail",
    )(x, *acts, p, *projs, fg, wg, wu, wd, pg, wp, wpg)


def _upproj_kernel(x_ref, g_ref, w_ref, xm_ref, z_ref):
    h = _rms(x_ref[...], g_ref[...]).astype(BF16)
    xm_ref[...] = jnp.dot(h, w_ref[:, :C_INNER], preferred_element_type=F32).astype(xm_ref.dtype)
    z_ref[...] = jnp.dot(h, w_ref[:, C_INNER:], preferred_element_type=F32).astype(z_ref.dtype)


def _upproj(x, g, w, tm):
    t = x.shape[0]
    return pl.pallas_call(
        _upproj_kernel,
        grid=(t // tm,),
        in_specs=[pl.BlockSpec((tm, D_MODEL), lambda i: (i, 0)), _resident((1, D_MODEL)), _resident(w.shape)],
        out_specs=[pl.BlockSpec((tm, C_INNER), lambda i: (i, 0))] * 2,
        out_shape=[jax.ShapeDtypeStruct((t, C_INNER), BF16)] * 2,
        compiler_params=_params("parallel"), name="upproj",
    )(x, g, w)


def _mlstm_kernel(xm_ref, z_ref, cw_ref, cb_ref, wq_ref, wk_ref, wv_ref, wif_ref, bif_ref, mg_ref, sk_ref,
                  y_ref, carry_scr, xc_o, q_o, k_o, v_o, h_scr, c_scr, n_scr, m_scr, *, lc):
    tm = lc
    c = pl.program_id(1)

    @pl.when(c == 0)
    def _():
        carry_scr[...] = jnp.zeros_like(carry_scr)
        c_scr[...] = jnp.zeros_like(c_scr)
        n_scr[...] = jnp.zeros_like(n_scr)
        m_scr[...] = jnp.zeros_like(m_scr)

    xmb = xm_ref[...]
    xm = xmb.astype(F32)
    carry = carry_scr[...]
    row8 = _iota((8, 1), 0)
    acc = cb_ref[...] + cw_ref[C_CONV - 1:C_CONV, :] * xm
    for sft in range(1, C_CONV):
        rolled = pltpu.roll(xm, sft, 0)
        head = jnp.where(row8 < sft, pltpu.roll(carry, sft, 0), rolled[:8])
        shifted = jnp.concatenate([head, rolled[8:]], axis=0)
        acc = acc + cw_ref[C_CONV - 1 - sft:C_CONV - sft, :] * shifted
    carry_scr[...] = xm[tm - 8:, :]
    xcb = _silu(acc).astype(BF16)
    xc_o[...] = xcb

    gates = jnp.zeros((tm, GATE_PAD), F32)
    for j in range(C_INNER // QKV_TILE):
        cols = slice(j * QKV_TILE, (j + 1) * QKV_TILE)
        qj = jnp.dot(xcb[:, cols], wq_ref[j], preferred_element_type=F32).astype(BF16)
        kj = jnp.dot(xcb[:, cols], wk_ref[j], preferred_element_type=F32).astype(BF16)
        vj = jnp.dot(xmb[:, cols], wv_ref[j], preferred_element_type=F32).astype(BF16)
        q_o[:, cols] = qj
        k_o[:, cols] = kj
        v_o[:, cols] = vj
        gates = gates + (jnp.dot(qj, wif_ref[0, cols, :], preferred_element_type=F32)
                         + jnp.dot(kj, wif_ref[1, cols, :], preferred_element_type=F32)
                         + jnp.dot(vj, wif_ref[2, cols, :], preferred_element_type=F32))
    gates = gates + bif_ref[...]
    lane = _iota((tm, GATE_PAD), 1)
    gates = jnp.where(lane < C_HEADS, gates, -_softplus(-gates))
    gcol = gates[:, :2 * C_HEADS]
    grow = gates.T[:2 * C_HEADS, :]

    d_h = C_HEAD_DIM
    rows = _iota((lc, lc), 0)
    cols = _iota((lc, lc), 1)
    tril = rows >= cols
    bcol = _dot_exact_lhs(jnp.where(tril, 1.0, 0.0), gcol)
    brow = _dot_exact_rhs(grow, jnp.where(rows <= cols, 1.0, 0.0))

    for h in range(C_HEADS):
        hs = slice(h * d_h, (h + 1) * d_h)
        b_c = bcol[:, C_HEADS + h:C_HEADS + h + 1]
        b_r = brow[C_HEADS + h:C_HEADS + h + 1, :]
        i_c = gcol[:, h:h + 1]
        i_r = grow[h:h + 1, :]
        m_prev = m_scr[h][0:1, 0:1]

        dmat = jnp.where(tril, b_c - b_r + i_r, -jnp.inf)
        inter = b_c + m_prev
        m_t = jnp.maximum(inter, jnp.max(dmat, axis=-1, keepdims=True))
        wts = jnp.exp(dmat - m_t)
        sc = jnp.exp(inter - m_t)

        qb = q_o[:, hs]
        qh = qb.astype(F32)
        kh = k_o[:, hs].astype(F32) * (d_h ** -0.5)
        vh = v_o[:, hs]
        qk = _dot_nt(qb, kh) * wts
        c_prev = c_scr[h]
        n_prev = n_scr[h][0:1, :]
        num = jnp.dot(qk.astype(BF16), vh, preferred_element_type=F32) + sc * _dot(qb, c_prev)
        den = jnp.sum(qk, axis=-1, keepdims=True) + sc * jnp.sum(qh * n_prev, axis=-1, keepdims=True)
        h_scr[:, hs] = num / jnp.maximum(jnp.abs(den), jnp.exp(-m_t))

        b_last = b_r[:, lc - 1:lc]
        m_new = jnp.maximum(b_last + m_prev, jnp.max(b_last - b_r + i_r, axis=-1, keepdims=True))
        dc = jnp.exp(b_last + m_prev - m_new)
        kws = kh * jnp.exp(b_last - b_c + i_c - m_new)
        c_scr[h] = dc * c_prev + _dot_tn(kws, vh)
        n_scr[h] = jnp.broadcast_to(dc * n_prev + jnp.sum(kws, axis=0, keepdims=True), (8, d_h))
        m_scr[h] = jnp.broadcast_to(m_new, (8, 128))

    for h in range(C_HEADS):
        hs = slice(h * d_h, (h + 1) * d_h)
        y = h_scr[:, hs]
        yc = y - jnp.mean(y, axis=-1, keepdims=True)
        var = jnp.mean(yc * yc, axis=-1, keepdims=True)
        hn = yc * lax.rsqrt(var + C_GN_EPS) * mg_ref[:, hs]
        y_ref[:, hs] = ((hn + sk_ref[:, hs] * xc_o[:, hs].astype(F32))
                        * _silu(z_ref[:, hs].astype(F32))).astype(y_ref.dtype)


def _mlstm(xm, z, cw, cb, wq, wk, wv, wif, bif, mh_g, skip, b, s, lc):
    t = b * s
    nc = s // lc
    row = pl.BlockSpec((lc, C_INNER), lambda bi, ci: (bi * nc + ci, 0))
    act = pltpu.VMEM((lc, C_INNER), BF16)
    return pl.pallas_call(
        functools.partial(_mlstm_kernel, lc=lc),
        grid=(b, nc),
        in_specs=[row, row, _resident(cw.shape), _resident(cb.shape), _resident(wq.shape), _resident(wk.shape),
                  _resident(wv.shape), _resident(wif.shape), _resident(bif.shape), _resident((1, C_INNER)),
                  _resident((1, C_INNER))],
        out_specs=row,
        out_shape=jax.ShapeDtypeStruct((t, C_INNER), BF16),
        scratch_shapes=[pltpu.VMEM((8, C_INNER), F32), act, act, act, act, pltpu.VMEM((lc, C_INNER), F32),
                        pltpu.VMEM((C_HEADS, C_HEAD_DIM, C_HEAD_DIM), F32),
                        pltpu.VMEM((C_HEADS, 8, C_HEAD_DIM), F32), pltpu.VMEM((C_HEADS, 8, 128), F32)],
        compiler_params=_params("parallel", "arbitrary"), name="mlstm",
    )(xm, z, cw, cb, wq, wk, wv, wif, bif, mh_g, skip)


def _pack_w_in(w_in):
    q, k, v, iq, ik, iw = jnp.split(w_in[:, :A_TOTAL], [512, 576, 640, 896, 960], axis=1)
    pad = jnp.zeros((D_MODEL, A_PAD - A_TOTAL), w_in.dtype)
    wa = jnp.concatenate([q, iq, k, v, ik, iw, pad], axis=1)
    return wa.astype(BF16), w_in[:, A_TOTAL:].astype(BF16)


def _block_diag_tiles(w):
    per = QKV_TILE // C_QKV_BLOCK
    wt = w.reshape(C_INNER // QKV_TILE, per, C_QKV_BLOCK, C_QKV_BLOCK)
    eye = jnp.eye(per, dtype=w.dtype)
    full = wt[:, :, :, None, :] * eye[None, :, None, :, None]
    return full.reshape(C_INNER // QKV_TILE, QKV_TILE, QKV_TILE).astype(BF16)


def _row_tile(s, want):
    return min(want, s)


def kernel(x, p, mix_norm, a_q_gain, a_k_gain, w_in_e, b_mu, b_w0, b_w2, b_a0, b_a2, b_g2, b_k_k, b_k_a,
           b_r_k, b_gn_g, b_gn_b, w_out_e, c_w_up, c_conv_w, c_conv_b, c_wq, c_wk, c_wv, c_w_if, c_b_i,
           c_b_f, c_mh_g, c_skip, c_w_down, ffn_norm, ffn_w_gate, ffn_w_up, ffn_w_down, ple_w, ple_norm,
           ple_w_gate):
    b, s, d = x.shape
    t = b * s
    depth = p.shape[0]
    tm = _row_tile(s, 512)
    xf = x.reshape(t, d)
    pf = p.reshape(depth, t, PLE_DIM)
    vec = lambda a: a.reshape(1, -1)

    for i in range(depth):
        j = i // 2
        if i % 2 == 0:
            wa, wb = _pack_w_in(w_in_e[j])
            ua, ub = _inproj(xf, vec(mix_norm[i]), wa, wb, tm)
            ya = _dsa(ua, vec(a_q_gain[j]), vec(a_k_gain[j]), b, s, _row_tile(s, 256))
            yb = _rwkv(ub, vec(b_mu[j]), vec(b_w0[j]), vec(b_a0[j]), vec(b_k_k[j]), vec(b_k_a[j]),
                       b_w2[j].astype(BF16), b_a2[j].astype(BF16), b_g2[j].astype(BF16), vec(b_r_k[j]),
                       vec(b_gn_g[j]), vec(b_gn_b[j]), b, s, CHUNK, 8)
            wo = w_out_e[j].astype(BF16)
            acts, projs = [ya, yb], [wo[:A_WIDTH], wo[A_WIDTH:]]
        else:
            xm, z = _upproj(xf, vec(mix_norm[i]), c_w_up[j].astype(BF16), tm)
            wif = jnp.pad(c_w_if[j], ((0, 0), (0, GATE_PAD - 2 * C_HEADS))).reshape(3, C_INNER, GATE_PAD)
            bif = jnp.pad(jnp.concatenate([c_b_i[j], c_b_f[j]]), (0, GATE_PAD - 2 * C_HEADS)).reshape(1, GATE_PAD)
            yc = _mlstm(xm, z, c_conv_w[j].reshape(C_CONV, C_INNER), vec(c_conv_b[j]), _block_diag_tiles(c_wq[j]),
                        _block_diag_tiles(c_wk[j]), _block_diag_tiles(c_wv[j]), wif.astype(BF16), bif,
                        vec(c_mh_g[j]), vec(c_skip[j]), b, s, _row_tile(s, 256))
            acts, projs = [yc], [c_w_down[j].astype(BF16)]
        xf = _tail(xf, acts, projs, pf[i], vec(ffn_norm[i]), ffn_w_gate[i].astype(BF16), ffn_w_up[i].astype(BF16),
                   ffn_w_down[i].astype(BF16), vec(ple_norm[i]), ple_w[i].astype(BF16),
                   ple_w_gate[i].astype(BF16), tm, 256)
    return xf.reshape(b, s, d)
```

```python
import functools
import math

import jax
import jax.numpy as jnp
from jax import lax
from jax.experimental import pallas as pl
from jax.experimental.pallas import tpu as pltpu

F32 = jnp.float32
BF16 = jnp.bfloat16

D_MODEL = 1024
CHUNK = 64
PLE_DIM = 256
NORM_EPS = 1e-6
A_HEADS = 8
A_HEAD_DIM = 64
A_KV_DIM = 64
A_WIDTH = A_HEADS * A_HEAD_DIM
IDX_HEADS = 4
IDX_DIM = 64
TOPK_MAX = 256
B_HEADS = 8
B_HEAD_DIM = 64
B_WIDTH = B_HEADS * B_HEAD_DIM
DECAY_LORA = 64
AAA_LORA = 64
GATE_LORA = 128
RWKV_GN_EPS = 64e-5
A_TOTAL = A_WIDTH + 2 * A_KV_DIM + IDX_HEADS * IDX_DIM + IDX_DIM + IDX_HEADS
B_TOTAL = 3 * B_WIDTH + DECAY_LORA + AAA_LORA + GATE_LORA
A_PAD = 1024
C_INNER = 2 * D_MODEL
C_HEADS = 4
C_HEAD_DIM = C_INNER // C_HEADS
C_CONV = 4
C_QKV_BLOCK = 4
C_GN_EPS = 1e-5
FFN_HIDDEN = 2816
MXU_TILE = 256
QKV_TILE = MXU_TILE
GATE_PAD = 128
RWKV_GROUP_HEADS = MXU_TILE // B_HEAD_DIM

VMEM_LIMIT = 56 * 1024 * 1024


def _params(*sem):
    return pltpu.CompilerParams(dimension_semantics=sem, vmem_limit_bytes=VMEM_LIMIT)


def _resident(shape):
    nd = len(shape)
    return pl.BlockSpec(shape, lambda *_: (0,) * nd, pipeline_mode=pl.Buffered(1))


def _rms(x, g):
    return x * lax.rsqrt(jnp.mean(x * x, axis=-1, keepdims=True) + NORM_EPS) * g


def _dot(a, b):
    return jnp.dot(a.astype(BF16), b.astype(BF16), preferred_element_type=F32)


def _dot_nt(a, b):
    return lax.dot_general(a.astype(BF16), b.astype(BF16), (((1,), (1,)), ((), ())),
                           preferred_element_type=F32)


def _dot_tn(a, b):
    return lax.dot_general(a.astype(BF16), b.astype(BF16), (((0,), (0,)), ((), ())),
                           preferred_element_type=F32)


def _split3(x):
    hi = x.astype(BF16)
    r1 = x - hi.astype(F32)
    mid = r1.astype(BF16)
    lo = (r1 - mid.astype(F32)).astype(BF16)
    return hi, mid, lo


def _dot_exact_lhs(a_exact, x):
    a = a_exact.astype(BF16)
    hi, mid, lo = _split3(x)
    return (jnp.dot(a, hi, preferred_element_type=F32) + jnp.dot(a, mid, preferred_element_type=F32)
            + jnp.dot(a, lo, preferred_element_type=F32))


def _dot_exact_rhs(x, b_exact):
    b = b_exact.astype(BF16)
    hi, mid, lo = _split3(x)
    return (jnp.dot(hi, b, preferred_element_type=F32) + jnp.dot(mid, b, preferred_element_type=F32)
            + jnp.dot(lo, b, preferred_element_type=F32))


def _group_sum(x, ones):
    hi = x.astype(BF16)
    lo = (x - hi.astype(F32)).astype(BF16)
    return jnp.dot(hi, ones, preferred_element_type=F32) + jnp.dot(lo, ones, preferred_element_type=F32)


def _sigmoid(x):
    return 1.0 / (1.0 + jnp.exp(-x))


def _softplus(x):
    return jnp.maximum(x, 0.0) + jnp.log(1.0 + jnp.exp(-jnp.abs(x)))


def _silu(x):
    return x * _sigmoid(x)


def _iota(shape, dim):
    return lax.broadcasted_iota(jnp.int32, shape, dim)


def _log2(n):
    assert n & (n - 1) == 0
    return n.bit_length() - 1


def _same_group(shape, group):
    sh = _log2(group)
    return jnp.right_shift(_iota(shape, 0), sh) == jnp.right_shift(_iota(shape, 1), sh)


def _inproj_kernel(x_ref, g_ref, wa_ref, wb_ref, ua_ref, ub_ref):
    h = _rms(x_ref[...], g_ref[...]).astype(BF16)
    ua_ref[...] = jnp.dot(h, wa_ref[...], preferred_element_type=F32)
    ub_ref[...] = jnp.dot(h, wb_ref[...], preferred_element_type=F32)


def _inproj(x, g, wa, wb, tm):
    t = x.shape[0]
    return pl.pallas_call(
        _inproj_kernel,
        grid=(t // tm,),
        in_specs=[pl.BlockSpec((tm, D_MODEL), lambda i: (i, 0)), _resident((1, D_MODEL)),
                  _resident(wa.shape), _resident(wb.shape)],
        out_specs=[pl.BlockSpec((tm, A_PAD), lambda i: (i, 0)),
                   pl.BlockSpec((tm, B_TOTAL), lambda i: (i, 0))],
        out_shape=[jax.ShapeDtypeStruct((t, A_PAD), F32), jax.ShapeDtypeStruct((t, B_TOTAL), F32)],
        compiler_params=_params("parallel"), name="inproj",
    )(x, g, wa, wb)


def _dsa_kernel(q_ref, iq_ref, iwq_ref, kv_ref, ikw_ref, qg_ref, kg_ref, out_ref, key_scr, sel_scr, kn_scr,
                v_scr, *, tq, k_sel):
    tk = tq
    qi = pl.program_id(1)
    nkt = qi + 1
    kf = jnp.float32(k_sel)
    needs_selection = (qi + 1) * tq > k_sel

    @pl.when(qi == 0)
    def _():
        kv = kv_ref[...]
        kraw = kv[:, :A_KV_DIM]
        kn_scr[...] = (kraw * lax.rsqrt(jnp.mean(kraw * kraw, axis=-1, keepdims=True) + NORM_EPS)
                       * kg_ref[...]).astype(BF16)
        v_scr[...] = kv[:, A_KV_DIM:2 * A_KV_DIM].astype(BF16)

    qpos = qi * tq + _iota((1, tq), 1)
    limit = (jnp.right_shift(qpos, _log2(CHUNK)) + 1) * CHUNK

    def valid_tile(kt):
        return kt * tk + _iota((tk, tq), 0) < limit

    def tile_rows(ref, kt):
        return ref[pl.ds(pl.multiple_of(kt * tk, tk), tk), :]

    @pl.when(needs_selection)
    def _():
        iq = iq_ref[...].astype(BF16)
        iq_h = [iq[:, h * IDX_DIM:(h + 1) * IDX_DIM] for h in range(IDX_HEADS)]
        pick = jnp.where((_iota((8, 128), 1) == _iota((8, 128), 0) + IDX_DIM) & (_iota((8, 128), 0) < IDX_HEADS),
                         1.0, 0.0).astype(BF16)
        iw_t = sum(lax.dot_general(pick, part, (((1,), (1,)), ((), ())), preferred_element_type=F32)
                   for part in _split3(iwq_ref[...])) * (IDX_HEADS ** -0.5 * IDX_DIM ** -0.5)

        def score_body(kt, _):
            ik_t = tile_rows(ikw_ref, kt)[:, :IDX_DIM].astype(BF16)
            sc = jnp.zeros((tk, tq), F32)
            for h in range(IDX_HEADS):
                sc = sc + iw_t[h:h + 1, :] * jnp.maximum(_dot_nt(ik_t, iq_h[h]), 0.0)
            sc = jnp.where(valid_tile(kt), sc + 0.0, -jnp.inf)
            bits = pltpu.bitcast(sc, jnp.int32)
            key_scr[kt] = jnp.where(bits < 0, bits ^ jnp.int32(0x7FFFFFFF), bits)
            return 0

        lax.fori_loop(0, nkt, score_body, 0)

        def count_ge(cand):
            def hits(kt):
                hit = jnp.where(key_scr[kt] >= cand, jnp.float32(1.0), jnp.float32(0.0))
                return jnp.sum(hit.reshape(tk // 8, 8, tq), axis=0)

            def body(i, acc):
                second = 2 * i + 1
                weight = jnp.where(second < nkt, jnp.float32(1.0), jnp.float32(0.0))
                return acc + hits(2 * i) + hits(jnp.minimum(second, nkt - 1)) * weight

            pairs = jnp.right_shift(nkt + 1, 1)
            return jnp.sum(lax.fori_loop(0, pairs, body, jnp.zeros((8, tq), F32)), axis=0, keepdims=True)

        ans = jnp.where(count_ge(jnp.zeros((1, tq), jnp.int32)) >= kf, jnp.int32(0), jnp.int32(-2 ** 31))

        def bit_step(i, ans):
            cand = ans | jnp.left_shift(jnp.int32(1), jnp.int32(30) - i)
            return jnp.where(count_ge(cand) >= kf, cand, ans)

        thr = lax.fori_loop(0, 31, bit_step, ans)
        need = kf - count_ge(thr + 1)
        tril = jnp.where(_iota((tk, tk), 0) >= _iota((tk, tk), 1), 1.0, 0.0).astype(BF16)

        def tie_body(kt, carry):
            key = key_scr[kt]
            eq = key == thr
            rank = carry + jnp.dot(tril, jnp.where(eq, 1.0, 0.0).astype(BF16), preferred_element_type=F32)
            sel = ((key > thr) | (eq & (rank <= need))) & valid_tile(kt)
            sel_scr[kt] = jnp.where(sel, 0.0, -jnp.inf)
            return rank[tk - 1:tk, :]

        lax.fori_loop(0, nkt, tie_body, jnp.zeros((1, tq), F32))

    @pl.when(jnp.logical_not(needs_selection))
    def _():
        def body(kt, _):
            sel_scr[kt] = jnp.where(valid_tile(kt), 0.0, -jnp.inf)
            return 0
        lax.fori_loop(0, nkt, body, 0)

    q = q_ref[...]
    qscale = A_HEAD_DIM ** -0.5 * math.log2(math.e)
    ones_g = jnp.where(_same_group((MXU_TILE, MXU_TILE), A_HEAD_DIM), 1.0, 0.0).astype(BF16)
    msq = jnp.concatenate([_group_sum(q[:, i * MXU_TILE:(i + 1) * MXU_TILE] ** 2, ones_g)
                           for i in range(A_WIDTH // MXU_TILE)], axis=1) * (1.0 / A_HEAD_DIM)
    qn_all = (q * lax.rsqrt(msq + NORM_EPS) * (qg_ref[...] * qscale)).astype(BF16)
    qn = [qn_all[:, h * A_HEAD_DIM:(h + 1) * A_HEAD_DIM] for h in range(A_HEADS)]

    def att_body(kt, carry):
        ms, ls, accs = carry
        kn_t = tile_rows(kn_scr, kt)
        v_t = tile_rows(v_scr, kt)
        neg = sel_scr[kt]
        sc = [_dot_nt(kn_t, qn[h]) + neg for h in range(A_HEADS)]
        m_new = [jnp.maximum(ms[h], jnp.max(sc[h], axis=0, keepdims=True)) for h in range(A_HEADS)]
        p = [jnp.exp2(sc[h] - m_new[h]) for h in range(A_HEADS)]
        alpha = [jnp.exp2(ms[h] - m_new[h]) for h in range(A_HEADS)]
        ls = [alpha[h] * ls[h] + jnp.sum(p[h], axis=0, keepdims=True) for h in range(A_HEADS)]
        accs = [alpha[h] * accs[h] + _dot_tn(v_t, p[h]) for h in range(A_HEADS)]
        return m_new, ls, accs

    init = ([jnp.full((1, tq), -1e30, F32)] * A_HEADS, [jnp.zeros((1, tq), F32)] * A_HEADS,
            [jnp.zeros((A_KV_DIM, tq), F32)] * A_HEADS)
    _, ls, accs = lax.fori_loop(0, nkt, att_body, init)
    out_t = jnp.concatenate([accs[h] / ls[h] for h in range(A_HEADS)], axis=0)
    out_ref[...] = out_t.T.astype(out_ref.dtype)


def _dsa(ua, q_gain, k_gain, b, s, tq):
    t = b * s
    nq = s // tq
    k_sel = min(TOPK_MAX, s // 4)
    kern = functools.partial(_dsa_kernel, tq=tq, k_sel=k_sel)
    return pl.pallas_call(
        kern,
        grid=(b, nq),
        in_specs=[
            pl.BlockSpec((tq, A_WIDTH), lambda bi, qi: (bi * nq + qi, 0)),
            pl.BlockSpec((tq, 256), lambda bi, qi: (bi * nq + qi, 2)),
            pl.BlockSpec((tq, 128), lambda bi, qi: (bi * nq + qi, 7)),
            pl.BlockSpec((s, 128), lambda bi, qi: (bi, 6)),
            pl.BlockSpec((s, 128), lambda bi, qi: (bi, 7)),
            _resident((1, A_WIDTH)), _resident((1, A_KV_DIM)),
        ],
        out_specs=pl.BlockSpec((tq, A_WIDTH), lambda bi, qi: (bi * nq + qi, 0)),
        out_shape=jax.ShapeDtypeStruct((t, A_WIDTH), BF16),
        scratch_shapes=[pltpu.VMEM((nq, tq, tq), jnp.int32), pltpu.VMEM((nq, tq, tq), F32),
                        pltpu.VMEM((s, A_KV_DIM), BF16), pltpu.VMEM((s, A_KV_DIM), BF16)],
        compiler_params=_params("parallel", "arbitrary"), name="dsa",
    )(ua, ua, ua, ua, ua, jnp.tile(q_gain, (1, A_HEADS)), k_gain)


def _rwkv_kernel(ub_ref, mu_ref, w0_ref, a0_ref, kkw_ref, ka_ref, w2_ref, a2_ref, g2_ref, rk_ref, gng_ref,
                 gnb_ref, out_ref, state_scr, carry_scr, y_scr, *, lc, nsub):
    n = B_HEAD_DIM
    assert lc == n
    wd = B_WIDTH
    gw = MXU_TILE
    ngrp = wd // gw
    rows = lc * nsub
    c = pl.program_id(1)

    @pl.when(c == 0)
    def _():
        state_scr[...] = jnp.zeros_like(state_scr)
        carry_scr[...] = jnp.zeros_like(carry_scr)

    u = ub_ref[...]
    prev = pltpu.roll(u, 1, 0)
    prev = jnp.where(_iota((rows, 1), 0) == 0, carry_scr[...], prev)
    carry_scr[...] = u[rows - 1:rows, :]
    us = u + mu_ref[...] * (prev - u)
    r = us[:, :wd]
    k_in = us[:, wd:2 * wd]
    v = us[:, 2 * wd:3 * wd]
    o = 3 * wd
    xw = us[:, o:o + DECAY_LORA]
    xa = us[:, o + DECAY_LORA:o + DECAY_LORA + AAA_LORA]
    xg = us[:, o + DECAY_LORA + AAA_LORA:]
    w_raw = -_softplus(-(w0_ref[...] + _dot(jnp.tanh(xw), w2_ref[...]))) - 0.5
    lw = -jnp.exp(w_raw)
    a = _sigmoid(a0_ref[...] + _dot(xa, a2_ref[...]))
    g = _dot(_sigmoid(xg), g2_ref[...])
    ones_g = jnp.where(_same_group((gw, gw), n), 1.0, 0.0).astype(BF16)
    kk = k_in * kkw_ref[...]
    nrm2 = jnp.concatenate([_group_sum(kk[:, i * gw:(i + 1) * gw] ** 2, ones_g) for i in range(ngrp)], axis=1)
    kk = kk / jnp.maximum(jnp.sqrt(nrm2), 1e-12)
    k = k_in * (1.0 + (a - 1.0) * ka_ref[...])

    ri = _iota((rows, rows), 0)
    ci = _iota((rows, rows), 1)
    incl_rows = _same_group((rows, rows), lc) & (ri >= ci)
    cum = _dot_exact_lhs(jnp.where(incl_rows, 1.0, 0.0), lw)
    gam = jnp.exp(cum)
    gam_inv = jnp.exp(-cum)
    at = -kk * jnp.exp(cum - lw)
    bt = kk * a * gam_inv
    kt = k * gam_inv
    rt = r * gam

    bdmask = _same_group((gw, gw), n)
    lane_s = _iota((lc, gw), 1) & (n - 1)
    row_t = _iota((lc, gw), 0)
    strict = row_t > lane_s
    incl = row_t >= lane_s
    eye = jnp.where(row_t == lane_s, 1.0, 0.0)

    def bd(x):
        return jnp.where(bdmask, jnp.concatenate([x] * (gw // lc), axis=0), 0.0).astype(BF16)

    def stack(top, bottom):
        return jnp.concatenate([top, bottom], axis=0).astype(BF16)

    insts = [(j, i) for j in range(nsub) for i in range(ngrp)]

    def blk(x, j, i):
        return x[j * lc:(j + 1) * lc, i * gw:(i + 1) * gw]

    a_ab, a_ak, a_qb, a_qk = {}, {}, {}, {}
    for ji in insts:
        ar = stack(blk(at, *ji), blk(rt, *ji))
        pb = _dot_nt(ar, bd(blk(bt, *ji)))
        pk = _dot_nt(ar, bd(blk(kt, *ji)))
        a_ab[ji] = jnp.where(strict, pb[:lc], 0.0)
        a_ak[ji] = jnp.where(strict, pk[:lc], 0.0)
        a_qb[ji] = jnp.where(incl, pb[lc:], 0.0)
        a_qk[ji] = jnp.where(incl, pk[lc:], 0.0)

    x = {ji: eye + a_ab[ji] for ji in insts}
    pw = {ji: _dot(a_ab[ji], bd(a_ab[ji])) for ji in insts}
    for _ in range(_log2(lc) - 2):
        for ji in insts:
            res = jnp.dot(stack(x[ji], pw[ji]), bd(pw[ji]), preferred_element_type=F32)
            x[ji] = x[ji] + res[:lc]
            pw[ji] = res[lc:]
    for ji in insts:
        x[ji] = x[ji] + _dot(x[ji], bd(pw[ji]))

    w_mat, u_v, y_intra, kv_new, be, gl = {}, {}, {}, {}, {}, {}
    for ji in insts:
        j, i = ji
        t2 = jnp.dot(stack(a_ak[ji], a_qk[ji]), bd(blk(v, *ji)), preferred_element_type=F32)
        y_intra[ji] = t2[lc:]
        w_mat[ji] = _dot(x[ji], bd(blk(at, *ji)))
        u_v[ji] = _dot(x[ji], bd(t2[:lc]))
        gl[ji] = gam[(j + 1) * lc - 1:(j + 1) * lc, i * gw:(i + 1) * gw]
        be[ji] = blk(bt, *ji) * gl[ji]
        kv_new[ji] = jnp.where(bdmask, _dot_tn(blk(v, *ji), blk(kt, *ji) * gl[ji]), 0.0)

    state = [state_scr[i] for i in range(ngrp)]
    for ji in insts:
        j, i = ji
        s0 = state[i]
        r2 = _dot_nt(stack(w_mat[ji], blk(rt, *ji)), s0)
        uu = r2[:lc] + u_v[ji]
        y_scr[j * lc:(j + 1) * lc, i * gw:(i + 1) * gw] = r2[lc:] + _dot(a_qb[ji], bd(uu)) + y_intra[ji]
        state[i] = s0 * gl[ji] + jnp.where(bdmask, _dot_tn(uu, be[ji]), 0.0) + kv_new[ji]
    for i in range(ngrp):
        state_scr[i] = state[i]

    for i in range(ngrp):
        cols = slice(i * gw, (i + 1) * gw)
        y = y_scr[:, cols]
        mean = _group_sum(y, ones_g) * (1.0 / n)
        yc = y - mean
        var = _group_sum(yc * yc, ones_g) * (1.0 / n)
        yn = yc * lax.rsqrt(var + RWKV_GN_EPS) * gng_ref[:, cols] + gnb_ref[:, cols]
        bonus = _group_sum(r[:, cols] * k[:, cols] * rk_ref[:, cols], ones_g) * v[:, cols]
        out_ref[:, cols] = ((yn + bonus) * g[:, cols]).astype(out_ref.dtype)


def _rwkv(ub, mu, w0, a0, k_k, k_a, w2, a2, g2, r_k, gn_g, gn_b, b, s, lc, nsub):
    t = b * s
    rows = lc * nsub
    assert s % rows == 0
    nc = s // rows
    kern = functools.partial(_rwkv_kernel, lc=lc, nsub=nsub)
    vec = _resident((1, B_WIDTH))
    ngrp = B_WIDTH // MXU_TILE
    return pl.pallas_call(
        kern,
        grid=(b, nc),
        in_specs=[pl.BlockSpec((rows, B_TOTAL), lambda bi, ci: (bi * nc + ci, 0)), _resident((1, B_TOTAL)),
                  vec, vec, vec, vec, _resident(w2.shape), _resident(a2.shape), _resident(g2.shape),
                  vec, vec, vec],
        out_specs=pl.BlockSpec((rows, B_WIDTH), lambda bi, ci: (bi * nc + ci, 0)),
        out_shape=jax.ShapeDtypeStruct((t, B_WIDTH), BF16),
        scratch_shapes=[pltpu.VMEM((ngrp, MXU_TILE, MXU_TILE), F32), pltpu.VMEM((1, B_TOTAL), F32),
                        pltpu.VMEM((rows, B_WIDTH), F32)],
        compiler_params=_params("parallel", "arbitrary"), name="rwkv",
    )(ub, mu, w0, a0, k_k, k_a, w2, a2, g2, r_k, gn_g, gn_b)


def _tail_kernel(*refs, n_proj, th):
    x_ref = refs[0]
    acts = refs[1:1 + n_proj]
    p_ref = refs[1 + n_proj]
    projs = refs[2 + n_proj:2 + 2 * n_proj]
    fg_ref, wg_ref, wu_ref, wd_ref, pg_ref, wp_ref, wpg_ref, o_ref = refs[2 + 2 * n_proj:]

    x = x_ref[...]
    for a_ref, w_ref in zip(acts, projs):
        x = x + jnp.dot(a_ref[...], w_ref[...], preferred_element_type=F32)
    o_ref[...] = x
    x = o_ref[...]

    h = _rms(x, fg_ref[...]).astype(BF16)
    for j in range(FFN_HIDDEN // th):
        cols = slice(j * th, (j + 1) * th)
        gate = jnp.dot(h, wg_ref[:, cols], preferred_element_type=F32)
        up = jnp.dot(h, wu_ref[:, cols], preferred_element_type=F32)
        x = x + jnp.dot((_silu(gate) * up).astype(BF16), wd_ref[cols, :], preferred_element_type=F32)

    gate = _sigmoid(jnp.dot(_rms(x, pg_ref[...]).astype(BF16), wpg_ref[...], preferred_element_type=F32))
    o_ref[...] = x + jnp.dot(p_ref[...].astype(BF16), wp_ref[...], preferred_element_type=F32) * gate


def _tail(x, acts, projs, p, fg, wg, wu, wd, pg, wp, wpg, tm, th):
    t = x.shape[0]
    rows = lambda width: pl.BlockSpec((tm, width), lambda i: (i, 0))
    vec = _resident((1, D_MODEL))
    return pl.pallas_call(
        functools.partial(_tail_kernel, n_proj=len(acts), th=th),
        grid=(t // tm,),
        in_specs=[rows(D_MODEL)] + [rows(a.shape[1]) for a in acts] + [rows(PLE_DIM)]
        + [_resident(w.shape) for w in projs]
        + [vec, _resident(wg.shape), _resident(wu.shape), _resident(wd.shape), vec, _resident(wp.shape),
           _resident(wpg.shape)],
        out_specs=rows(D_MODEL),
        out_shape=jax.ShapeDtypeStruct((t, D_MODEL), F32),
        compiler_params=_params("parallel"), name="tail",
    )(x, *acts, p, *projs, fg, wg, wu, wd, pg, wp, wpg)


def _upproj_kernel(x_ref, g_ref, w_ref, xm_ref, z_ref):
    h = _rms(x_ref[...], g_ref[...]).astype(BF16)
    xm_ref[...] = jnp.dot(h, w_ref[:, :C_INNER], preferred_element_type=F32).astype(xm_ref.dtype)
    z_ref[...] = jnp.dot(h, w_ref[:, C_INNER:], preferred_element_type=F32).astype(z_ref.dtype)


def _upproj(x, g, w, tm):
    t = x.shape[0]
    return pl.pallas_call(
        _upproj_kernel,
        grid=(t // tm,),
        in_specs=[pl.BlockSpec((tm, D_MODEL), lambda i: (i, 0)), _resident((1, D_MODEL)), _resident(w.shape)],
        out_specs=[pl.BlockSpec((tm, C_INNER), lambda i: (i, 0))] * 2,
        out_shape=[jax.ShapeDtypeStruct((t, C_INNER), BF16)] * 2,
        compiler_params=_params("parallel"), name="upproj",
    )(x, g, w)


def _mlstm_kernel(xm_ref, z_ref, cw_ref, cb_ref, wq_ref, wk_ref, wv_ref, wif_ref, bif_ref, mg_ref, sk_ref,
                  y_ref, carry_scr, xc_o, q_o, k_o, v_o, h_scr, c_scr, n_scr, m_scr, *, lc):
    tm = lc
    c = pl.program_id(1)

    @pl.when(c == 0)
    def _():
        carry_scr[...] = jnp.zeros_like(carry_scr)
        c_scr[...] = jnp.zeros_like(c_scr)
        n_scr[...] = jnp.zeros_like(n_scr)
        m_scr[...] = jnp.zeros_like(m_scr)

    xmb = xm_ref[...]
    xm = xmb.astype(F32)
    carry = carry_scr[...]
    row8 = _iota((8, 1), 0)
    acc = cb_ref[...] + cw_ref[C_CONV - 1:C_CONV, :] * xm
    for sft in range(1, C_CONV):
        rolled = pltpu.roll(xm, sft, 0)
        head = jnp.where(row8 < sft, pltpu.roll(carry, sft, 0), rolled[:8])
        shifted = jnp.concatenate([head, rolled[8:]], axis=0)
        acc = acc + cw_ref[C_CONV - 1 - sft:C_CONV - sft, :] * shifted
    carry_scr[...] = xm[tm - 8:, :]
    xcb = _silu(acc).astype(BF16)
    xc_o[...] = xcb

    gates = jnp.zeros((tm, GATE_PAD), F32)
    for j in range(C_INNER // QKV_TILE):
        cols = slice(j * QKV_TILE, (j + 1) * QKV_TILE)
        qj = jnp.dot(xcb[:, cols], wq_ref[j], preferred_element_type=F32).astype(BF16)
        kj = jnp.dot(xcb[:, cols], wk_ref[j], preferred_element_type=F32).astype(BF16)
        vj = jnp.dot(xmb[:, cols], wv_ref[j], preferred_element_type=F32).astype(BF16)
        q_o[:, cols] = qj
        k_o[:, cols] = kj
        v_o[:, cols] = vj
        gates = gates + (jnp.dot(qj, wif_ref[0, cols, :], preferred_element_type=F32)
                         + jnp.dot(kj, wif_ref[1, cols, :], preferred_element_type=F32)
                         + jnp.dot(vj, wif_ref[2, cols, :], preferred_element_type=F32))
    gates = gates + bif_ref[...]
    lane = _iota((tm, GATE_PAD), 1)
    gates = jnp.where(lane < C_HEADS, gates, -_softplus(-gates))
    gcol = gates[:, :2 * C_HEADS]
    grow = gates.T[:2 * C_HEADS, :]

    d_h = C_HEAD_DIM
    rows = _iota((lc, lc), 0)
    cols = _iota((lc, lc), 1)
    tril = rows >= cols
    bcol = _dot_exact_lhs(jnp.where(tril, 1.0, 0.0), gcol)
    brow = _dot_exact_rhs(grow, jnp.where(rows <= cols, 1.0, 0.0))

    for h in range(C_HEADS):
        hs = slice(h * d_h, (h + 1) * d_h)
        b_c = bcol[:, C_HEADS + h:C_HEADS + h + 1]
        b_r = brow[C_HEADS + h:C_HEADS + h + 1, :]
        i_c = gcol[:, h:h + 1]
        i_r = grow[h:h + 1, :]
        m_prev = m_scr[h][0:1, 0:1]

        dmat = jnp.where(tril, b_c - b_r + i_r, -jnp.inf)
        inter = b_c + m_prev
        m_t = jnp.maximum(inter, jnp.max(dmat, axis=-1, keepdims=True))
        wts = jnp.exp(dmat - m_t)
        sc = jnp.exp(inter - m_t)

        qb = q_o[:, hs]
        qh = qb.astype(F32)
        kh = k_o[:, hs].astype(F32) * (d_h ** -0.5)
        vh = v_o[:, hs]
        qk = _dot_nt(qb, kh) * wts
        c_prev = c_scr[h]
        n_prev = n_scr[h][0:1, :]
        num = jnp.dot(qk.astype(BF16), vh, preferred_element_type=F32) + sc * _dot(qb, c_prev)
        den = jnp.sum(qk, axis=-1, keepdims=True) + sc * jnp.sum(qh * n_prev, axis=-1, keepdims=True)
        h_scr[:, hs] = num / jnp.maximum(jnp.abs(den), jnp.exp(-m_t))

        b_last = b_r[:, lc - 1:lc]
        m_new = jnp.maximum(b_last + m_prev, jnp.max(b_last - b_r + i_r, axis=-1, keepdims=True))
        dc = jnp.exp(b_last + m_prev - m_new)
        kws = kh * jnp.exp(b_last - b_c + i_c - m_new)
        c_scr[h] = dc * c_prev + _dot_tn(kws, vh)
        n_scr[h] = jnp.broadcast_to(dc * n_prev + jnp.sum(kws, axis=0, keepdims=True), (8, d_h))
        m_scr[h] = jnp.broadcast_to(m_new, (8, 128))

    for h in range(C_HEADS):
        hs = slice(h * d_h, (h + 1) * d_h)
        y = h_scr[:, hs]
        yc = y - jnp.mean(y, axis=-1, keepdims=True)
        var = jnp.mean(yc * yc, axis=-1, keepdims=True)
        hn = yc * lax.rsqrt(var + C_GN_EPS) * mg_ref[:, hs]
        y_ref[:, hs] = ((hn + sk_ref[:, hs] * xc_o[:, hs].astype(F32))
                        * _silu(z_ref[:, hs].astype(F32))).astype(y_ref.dtype)


def _mlstm(xm, z, cw, cb, wq, wk, wv, wif, bif, mh_g, skip, b, s, lc):
    t = b * s
    nc = s // lc
    row = pl.BlockSpec((lc, C_INNER), lambda bi, ci: (bi * nc + ci, 0))
    act = pltpu.VMEM((lc, C_INNER), BF16)
    return pl.pallas_call(
        functools.partial(_mlstm_kernel, lc=lc),
        grid=(b, nc),
        in_specs=[row, row, _resident(cw.shape), _resident(cb.shape), _resident(wq.shape), _resident(wk.shape),
                  _resident(wv.shape), _resident(wif.shape), _resident(bif.shape), _resident((1, C_INNER)),
                  _resident((1, C_INNER))],
        out_specs=row,
        out_shape=jax.ShapeDtypeStruct((t, C_INNER), BF16),
        scratch_shapes=[pltpu.VMEM((8, C_INNER), F32), act, act, act, act, pltpu.VMEM((lc, C_INNER), F32),
                        pltpu.VMEM((C_HEADS, C_HEAD_DIM, C_HEAD_DIM), F32),
                        pltpu.VMEM((C_HEADS, 8, C_HEAD_DIM), F32), pltpu.VMEM((C_HEADS, 8, 128), F32)],
        compiler_params=_params("parallel", "arbitrary"), name="mlstm",
    )(xm, z, cw, cb, wq, wk, wv, wif, bif, mh_g, skip)


def _pack_w_in(w_in):
    q, k, v, iq, ik, iw = jnp.split(w_in[:, :A_TOTAL], [512, 576, 640, 896, 960], axis=1)
    pad = jnp.zeros((D_MODEL, A_PAD - A_TOTAL), w_in.dtype)
    wa = jnp.concatenate([q, iq, k, v, ik, iw, pad], axis=1)
    return wa.astype(BF16), w_in[:, A_TOTAL:].astype(BF16)


def _block_diag_tiles(w):
    per = QKV_TILE // C_QKV_BLOCK
    wt = w.reshape(C_INNER // QKV_TILE, per, C_QKV_BLOCK, C_QKV_BLOCK)
    eye = jnp.eye(per, dtype=w.dtype)
    full = wt[:, :, :, None, :] * eye[None, :, None, :, None]
    return full.reshape(C_INNER // QKV_TILE, QKV_TILE, QKV_TILE).astype(BF16)


def _row_tile(s, want):
    return min(want, s)


def kernel(x, p, mix_norm, a_q_gain, a_k_gain, w_in_e, b_mu, b_w0, b_w2, b_a0, b_a2, b_g2, b_k_k, b_k_a,
           b_r_k, b_gn_g, b_gn_b, w_out_e, c_w_up, c_conv_w, c_conv_b, c_wq, c_wk, c_wv, c_w_if, c_b_i,
           c_b_f, c_mh_g, c_skip, c_w_down, ffn_norm, ffn_w_gate, ffn_w_up, ffn_w_down, ple_w, ple_norm,
           ple_w_gate):
    b, s, d = x.shape
    t = b * s
    depth = p.shape[0]
    tm = _row_tile(s, 512)
    xf = x.reshape(t, d)
    pf = p.reshape(depth, t, PLE_DIM)
    vec = lambda a: a.reshape(1, -1)

    for i in range(depth):
        j = i // 2
        if i % 2 == 0:
            wa, wb = _pack_w_in(w_in_e[j])
            ua, ub = _inproj(xf, vec(mix_norm[i]), wa, wb, tm)
            ya = _dsa(ua, vec(a_q_gain[j]), vec(a_k_gain[j]), b, s, _row_tile(s, 512))
            yb = _rwkv(ub, vec(b_mu[j]), vec(b_w0[j]), vec(b_a0[j]), vec(b_k_k[j]), vec(b_k_a[j]),
                       b_w2[j].astype(BF16), b_a2[j].astype(BF16), b_g2[j].astype(BF16), vec(b_r_k[j]),
                       vec(b_gn_g[j]), vec(b_gn_b[j]), b, s, CHUNK, 8)
            wo = w_out_e[j].astype(BF16)
            acts, projs = [ya, yb], [wo[:A_WIDTH], wo[A_WIDTH:]]
        else:
            xm, z = _upproj(xf, vec(mix_norm[i]), c_w_up[j].astype(BF16), tm)
            wif = jnp.pad(c_w_if[j], ((0, 0), (0, GATE_PAD - 2 * C_HEADS))).reshape(3, C_INNER, GATE_PAD)
            bif = jnp.pad(jnp.concatenate([c_b_i[j], c_b_f[j]]), (0, GATE_PAD - 2 * C_HEADS)).reshape(1, GATE_PAD)
            yc = _mlstm(xm, z, c_conv_w[j].reshape(C_CONV, C_INNER), vec(c_conv_b[j]), _block_diag_tiles(c_wq[j]),
                        _block_diag_tiles(c_wk[j]), _block_diag_tiles(c_wv[j]), wif.astype(BF16), bif,
                        vec(c_mh_g[j]), vec(c_skip[j]), b, s, _row_tile(s, 256))
            acts, projs = [yc], [c_w_down[j].astype(BF16)]
        xf = _tail(xf, acts, projs, pf[i], vec(ffn_norm[i]), ffn_w_gate[i].astype(BF16), ffn_w_up[i].astype(BF16),
                   ffn_w_down[i].astype(BF16), vec(ple_norm[i]), ple_w[i].astype(BF16),
                   ple_w_gate[i].astype(BF16), tm, 256)
    return xf.reshape(b, s, d)
```
